```python
import math
import jax
import jax.numpy as jnp
from jax import lax
import numpy as np

D_MODEL = 2048
BATCH = 4
SEQ = 4096
DEPTH = 4

ROPE_DIM = 64
ROPE_THETA = 10000.0
Q_BLOCK = 128

DA_HEADS = 8
DA_HEAD_DIM = ROPE_DIM
DA_WIDTH = DA_HEADS * 2 * DA_HEAD_DIM

HY_WIDTH = D_MODEL // 2
HY_ORDER = 2
HY_SHORT = 3
HY_EMB = 33
HY_FILTER_WIDTH = 64
HY_DECAY_TARGET = 1e-2
HY_FAST_DECAY = 0.3
HY_SLOW_DECAY = 1.5

MLA_HEADS = 8
MLA_NOPE = 128
MLA_ROPE = ROPE_DIM
MLA_V = 128
MLA_Q_RANK = D_MODEL // 4
MLA_KV_RANK = D_MODEL // 8
MLA_WIDTH = MLA_HEADS * MLA_V

N_BRANCH = 3
D_FF = 4 * D_MODEL
DEEPNORM_ALPHA = (2 * DEPTH) ** 0.25
DEEPNORM_BETA = (8 * DEPTH) ** -0.25
LN_EPS = 1e-5
RMS_EPS = 1e-6

IN_SPLITS = (DA_WIDTH, DA_WIDTH, DA_WIDTH, 3 * HY_WIDTH, MLA_Q_RANK, MLA_KV_RANK, MLA_ROPE, N_BRANCH * D_MODEL)
D_IN = sum(IN_SPLITS)

kernel_name = 'hybrid_diffattn_hyena_mla_encoder'


def layer_norm(x, g, b):
    xf = x.astype(jnp.float32)
    mu = jnp.mean(xf, -1, keepdims=True)
    var = jnp.mean(jnp.square(xf - mu), -1, keepdims=True)
    return ((xf - mu) * lax.rsqrt(var + LN_EPS) * g + b).astype(x.dtype)


def rms_norm(x, g, eps=RMS_EPS):
    xf = x.astype(jnp.float32)
    y = xf * lax.rsqrt(jnp.mean(xf * xf, -1, keepdims=True) + eps) * g
    return y.astype(x.dtype)


def rope_tables(positions, dim):
    inv = ROPE_THETA ** (-jnp.arange(0, dim, 2, dtype=jnp.float32) / dim)
    ang = positions.astype(jnp.float32)[..., None] * inv
    return jnp.cos(ang), jnp.sin(ang)


def apply_rope(x, cos, sin):
    extra = x.ndim - 3
    shape = cos.shape[:2] + (1,) * extra + cos.shape[-1:]
    c = cos.reshape(shape)
    s = sin.reshape(shape)
    x1, x2 = jnp.split(x.astype(jnp.float32), 2, axis=-1)
    return jnp.concatenate([x1 * c - x2 * s, x2 * c + x1 * s], axis=-1).astype(x.dtype)


def to_blocks(t):
    b, s = t.shape[:2]
    t = t.reshape((b, s // Q_BLOCK, Q_BLOCK) + t.shape[2:])
    return jnp.moveaxis(t, 1, 0)


def from_blocks(t):
    t = jnp.moveaxis(t, 0, 1)
    return t.reshape((t.shape[0], -1) + t.shape[3:])


def diff_attention(u_q, u_k, u_v, cos, sin, lam_vecs, subln_g, lambda_init):
    b, s, _ = u_q.shape
    q = apply_rope(u_q.reshape(b, s, DA_HEADS, 2, DA_HEAD_DIM), cos, sin)
    k = apply_rope(u_k.reshape(b, s, DA_HEADS, 2, DA_HEAD_DIM), cos, sin)
    v = u_v.reshape(b, s, DA_HEADS, 2 * DA_HEAD_DIM)
    lq1, lk1, lq2, lk2 = lam_vecs.astype(jnp.float32)
    lam = jnp.exp(jnp.sum(lq1 * lk1)) - jnp.exp(jnp.sum(lq2 * lk2)) + lambda_init
    scale = DA_HEAD_DIM ** -0.5

    def block(qb):
        sc = jnp.einsum('bqhcd,bkhcd->bhcqk', qb, k).astype(jnp.float32) * scale
        p = jax.nn.softmax(sc, axis=-1)
        w = p[:, :, 0] - lam * p[:, :, 1]
        return jnp.einsum('bhqk,bkhe->bqhe', w.astype(v.dtype), v)

    o = from_blocks(lax.map(block, to_blocks(q)))
    o = rms_norm(o, subln_g, LN_EPS) * (1.0 - lambda_init)
    return o.reshape(b, s, DA_WIDTH)


def mla_attention(c_q, c_kv, k_rope, cos, sin, q_norm_g, kv_norm_g, w_uq, w_ukv):
    b, s, _ = c_q.shape
    q = (rms_norm(c_q, q_norm_g) @ w_uq).reshape(b, s, MLA_HEADS, MLA_NOPE + MLA_ROPE)
    q_nope, q_rope = q[..., :MLA_NOPE], apply_rope(q[..., MLA_NOPE:], cos, sin)
    kv = (rms_norm(c_kv, kv_norm_g) @ w_ukv).reshape(b, s, MLA_HEADS, MLA_NOPE + MLA_V)
    k_nope, v = kv[..., :MLA_NOPE], kv[..., MLA_NOPE:]
    k_r = apply_rope(k_rope, cos, sin)
    scale = (MLA_NOPE + MLA_ROPE) ** -0.5

    def block(qs):
        qn, qr = qs
        sc = jnp.einsum('bqhd,bkhd->bhqk', qn, k_nope) + jnp.einsum('bqhd,bkd->bhqk', qr, k_r)
        p = jax.nn.softmax(sc.astype(jnp.float32) * scale, axis=-1)
        return jnp.einsum('bhqk,bkhd->bqhd', p.astype(v.dtype), v)

    o = from_blocks(lax.map(block, (to_blocks(q_nope), to_blocks(q_rope))))
    return o.reshape(b, s, MLA_WIDTH)


def hyena_features(seq_len):
    t = jnp.linspace(0.0, 1.0, seq_len, dtype=jnp.float32)[:, None]
    bands = (HY_EMB - 1) // 2
    w = 2.0 * math.pi * jnp.arange(seq_len, dtype=jnp.float32)[:, None] / seq_len
    f = jnp.linspace(1e-4, bands - 1, bands, dtype=jnp.float32)[None, :]
    ang = f * w
    z = jnp.concatenate([t, jnp.cos(ang), -jnp.sin(ang)], axis=-1)
    deltas = jnp.linspace(math.log(HY_DECAY_TARGET) / HY_FAST_DECAY,
                          math.log(HY_DECAY_TARGET) / HY_SLOW_DECAY, HY_WIDTH, dtype=jnp.float32)
    decay = jnp.exp(-t * jnp.abs(deltas)[None, :])
    return z, decay


def hyena_filter_spectrum(z, decay, w1, b1, w2, b2, w3, b3, freq, wout):
    seq_len = z.shape[0]
    h = jnp.sin(freq * (z @ w1 + b1))
    h = jnp.sin(freq * (h @ w2 + b2))
    h = jnp.sin(freq * (h @ w3 + b3))
    h = (h @ wout).astype(jnp.float32).reshape(seq_len, HY_ORDER, 2, HY_WIDTH) * decay[:, None, None, :]
    fwd, bwd = h[:, :, 0], h[:, :, 1]
    two_sided = jnp.concatenate([fwd[:1] + bwd[:1], fwd[1:],
                                 jnp.zeros_like(fwd[:1]), bwd[1:][::-1]], axis=0)
    two_sided = two_sided / jnp.sum(jnp.abs(two_sided), axis=0, keepdims=True)
    return jnp.fft.rfft(two_sided, axis=0)


def long_conv(u, k_spec, skip):
    seq_len = u.shape[1]
    u_spec = jnp.fft.rfft(u, n=2 * seq_len, axis=1)
    y = jnp.fft.irfft(u_spec * k_spec, n=2 * seq_len, axis=1)[:, :seq_len]
    return y + u * skip.astype(jnp.float32)


def hyena_mixer(u3, conv_w, conv_b, k_spec, skip):
    c = u3.shape[-1]
    uc = lax.conv_general_dilated(u3, conv_w[:, None, :], window_strides=(1,), padding='SAME',
                                  dimension_numbers=('NWC', 'WIO', 'NWC'),
                                  feature_group_count=c) + conv_b
    v, x1, x2 = jnp.split(uc, 3, axis=-1)
    z = v.astype(jnp.float32)
    for n, gate in enumerate((x1, x2)):
        z = gate.astype(jnp.float32) * long_conv(z, k_spec[:, n], skip[n])
    return z.astype(u3.dtype)


def setup_inputs(seed: int = 0) -> dict:
    key = jax.random.key(seed)
    ks = iter(jax.random.split(key, 40))

    def nrm(shape, scale):
        return jax.random.normal(next(ks), shape, jnp.float32) * scale

    def gain(shape):
        return 1.0 + nrm(shape, 0.02)

    L = DEPTH
    F = HY_FILTER_WIDTH
    return {
        'x': nrm((BATCH, SEQ, D_MODEL), 1.0),
        'positions': jnp.broadcast_to(jnp.arange(SEQ, dtype=jnp.int32), (BATCH, SEQ)),
        'ln_emb_g': gain((D_MODEL,)),
        'ln_emb_b': nrm((D_MODEL,), 0.02),
        'w_in': nrm((L, D_MODEL, D_IN), D_MODEL ** -0.5),
        'gate_b': nrm((L, N_BRANCH * D_MODEL), 0.02),
        'da_lambda': nrm((L, 4, DA_HEAD_DIM), 0.1),
        'da_subln_g': gain((L, 2 * DA_HEAD_DIM)),
        'hy_conv_w': nrm((L, HY_SHORT, 3 * HY_WIDTH), HY_SHORT ** -0.5),
        'hy_conv_b': nrm((L, 3 * HY_WIDTH), 0.02),
        'hy_f_w1': nrm((L, HY_EMB, F), HY_EMB ** -0.5),
        'hy_f_b1': nrm((L, F), 0.02),
        'hy_f_w2': nrm((L, F, F), F ** -0.5),
        'hy_f_b2': nrm((L, F), 0.02),
        'hy_f_w3': nrm((L, F, F), F ** -0.5),
        'hy_f_b3': nrm((L, F), 0.02),
        'hy_f_freq': gain((L, F)),
        'hy_f_wout': nrm((L, F, HY_ORDER * 2 * HY_WIDTH), F ** -0.5),
        'hy_skip': nrm((L, HY_ORDER, HY_WIDTH), 1.0),
        'mla_q_norm_g': gain((L, MLA_Q_RANK)),
        'mla_kv_norm_g': gain((L, MLA_KV_RANK)),
        'mla_w_uq': nrm((L, MLA_Q_RANK, MLA_HEADS * (MLA_NOPE + MLA_ROPE)), MLA_Q_RANK ** -0.5),
        'mla_w_ukv': nrm((L, MLA_KV_RANK, MLA_HEADS * (MLA_NOPE + MLA_V)), MLA_KV_RANK ** -0.5),
        'w_branch_a': nrm((L, DA_WIDTH, D_MODEL), DA_WIDTH ** -0.5),
        'w_branch_b': nrm((L, HY_WIDTH, D_MODEL), HY_WIDTH ** -0.5),
        'w_branch_c': nrm((L, MLA_WIDTH, D_MODEL), MLA_WIDTH ** -0.5),
        'w_out': nrm((L, D_MODEL, D_MODEL), D_MODEL ** -0.5 * DEEPNORM_BETA),
        'ln1_g': gain((L, D_MODEL)),
        'ln1_b': nrm((L, D_MODEL), 0.02),
        'mlp_w1': nrm((L, D_MODEL, D_FF), D_MODEL ** -0.5),
        'mlp_w2': nrm((L, D_FF, D_MODEL), D_FF ** -0.5 * DEEPNORM_BETA),
        'ln2_g': gain((L, D_MODEL)),
        'ln2_b': nrm((L, D_MODEL), 0.02),
    }


def reference(x, positions, ln_emb_g, ln_emb_b, w_in, gate_b, da_lambda, da_subln_g,
              hy_conv_w, hy_conv_b, hy_f_w1, hy_f_b1, hy_f_w2, hy_f_b2, hy_f_w3, hy_f_b3,
              hy_f_freq, hy_f_wout, hy_skip, mla_q_norm_g, mla_kv_norm_g, mla_w_uq, mla_w_ukv,
              w_branch_a, w_branch_b, w_branch_c, w_out, ln1_g, ln1_b, mlp_w1, mlp_w2,
              ln2_g, ln2_b):
    b, s, d = x.shape
    split_points = np.cumsum(IN_SPLITS)[:-1].tolist()
    cos, sin = rope_tables(positions, ROPE_DIM)
    z_pos, decay = hyena_features(s)
    h = layer_norm(x, ln_emb_g, ln_emb_b)
    for l in range(DEPTH):
        lambda_init = 0.8 - 0.6 * math.exp(-0.3 * l)
        proj = h @ w_in[l]
        qa, ka, va, u_hy, c_q, c_kv, k_rope, g = jnp.split(proj, split_points, axis=-1)

        o_a = diff_attention(qa, ka, va, cos, sin, da_lambda[l], da_subln_g[l], lambda_init)
        k_spec = hyena_filter_spectrum(z_pos, decay, hy_f_w1[l], hy_f_b1[l], hy_f_w2[l], hy_f_b2[l],
                                       hy_f_w3[l], hy_f_b3[l], hy_f_freq[l], hy_f_wout[l])
        o_b = hyena_mixer(u_hy, hy_conv_w[l], hy_conv_b[l], k_spec, hy_skip[l])
        o_c = mla_attention(c_q, c_kv, k_rope, cos, sin, mla_q_norm_g[l], mla_kv_norm_g[l],
                            mla_w_uq[l], mla_w_ukv[l])

        gates = jax.nn.sigmoid(g + gate_b[l]).reshape(b, s, N_BRANCH, d)
        merged = (gates[:, :, 0] * (o_a @ w_branch_a[l])
                  + gates[:, :, 1] * (o_b @ w_branch_b[l])
                  + gates[:, :, 2] * (o_c @ w_branch_c[l]))
        h = layer_norm(DEEPNORM_ALPHA * h + merged @ w_out[l], ln1_g[l], ln1_b[l])

        ff = jnp.square(jax.nn.relu(h @ mlp_w1[l])) @ mlp_w2[l]
        h = layer_norm(DEEPNORM_ALPHA * h + ff, ln2_g[l], ln2_b[l])
    return h
```

```python
import functools
import math

import jax
import jax.numpy as jnp
from jax import lax
from jax.experimental import pallas as pl
from jax.experimental.pallas import tpu as pltpu

D_MODEL = 2048
DEPTH = 4
ROPE_DIM = 64
ROPE_THETA = 10000.0
DA_HEADS = 8
DA_HEAD_DIM = 64
DA_WIDTH = DA_HEADS * 2 * DA_HEAD_DIM
HY_WIDTH = D_MODEL // 2
HY_ORDER = 2
HY_EMB = 33
HY_FILTER_WIDTH = 64
HY_DECAY_TARGET = 1e-2
HY_FAST_DECAY = 0.3
HY_SLOW_DECAY = 1.5
MLA_HEADS = 8
MLA_NOPE = 128
MLA_ROPE = 64
MLA_V = 128
MLA_Q_RANK = D_MODEL // 4
MLA_KV_RANK = D_MODEL // 8
MLA_WIDTH = MLA_HEADS * MLA_V
N_BRANCH = 3
D_FF = 4 * D_MODEL
DEEPNORM_ALPHA = (2 * DEPTH) ** 0.25
LN_EPS = 1e-5
RMS_EPS = 1e-6

LANES = 128
MLA_QK_PAD = 2 * LANES
VMEM_LIMIT_BYTES = 56 * 1024 * 1024

BF16 = jnp.bfloat16
F32 = jnp.float32


def _params(*semantics):
    return pltpu.CompilerParams(dimension_semantics=semantics,
                                vmem_limit_bytes=VMEM_LIMIT_BYTES)


def _dot(a, b):
    return jnp.dot(a, b, preferred_element_type=F32)


def _dot_nt(a, b):
    return lax.dot_general(a, b, (((1,), (1,)), ((), ())), preferred_element_type=F32)


def _layer_norm(v, g, b):
    mu = jnp.mean(v, -1, keepdims=True)
    d = v - mu
    var = jnp.mean(d * d, -1, keepdims=True)
    return d * lax.rsqrt(var + LN_EPS) * g + b


def _rope_lanes(x, c, sa, sb):
    return x * c + pltpu.roll(x, LANES - 32, 1) * sa + pltpu.roll(x, 32, 1) * sb


def _ln_kernel(x_ref, g_ref, b_ref, o_ref, ob_ref):
    y = _layer_norm(x_ref[...], g_ref[...], b_ref[...])
    o_ref[...] = y
    ob_ref[...] = y.astype(BF16)


def _entry_ln(x, g, b, tm=512):
    n, d = x.shape
    row = pl.BlockSpec((tm, d), lambda i: (i, 0))
    vec = pl.BlockSpec((1, d), lambda i: (0, 0))
    return pl.pallas_call(
        _ln_kernel, grid=(n // tm,), in_specs=[row, vec, vec], out_specs=[row, row],
        out_shape=[jax.ShapeDtypeStruct((n, d), F32), jax.ShapeDtypeStruct((n, d), BF16)],
        compiler_params=_params("parallel"), name="entry_ln")(x, g.reshape(1, d), b.reshape(1, d))


def _mm_kernel(a_ref, b_ref, o_ref, *, square_relu):
    acc = _dot(a_ref[...], b_ref[...])
    if square_relu:
        acc = jnp.square(jnp.maximum(acc, 0.0))
    o_ref[...] = acc.astype(o_ref.dtype)


def _matmul(a, b, out_dtype, tm, tn, square_relu=False, name="matmul"):
    m, k = a.shape
    _, n = b.shape
    return pl.pallas_call(
        functools.partial(_mm_kernel, square_relu=square_relu),
        grid=(m // tm, n // tn),
        in_specs=[pl.BlockSpec((tm, k), lambda i, j: (i, 0)),
                  pl.BlockSpec((k, tn), lambda i, j: (0, j))],
        out_specs=pl.BlockSpec((tm, tn), lambda i, j: (i, j)),
        out_shape=jax.ShapeDtypeStruct((m, n), out_dtype),
        compiler_params=_params("parallel", "parallel"), name=name)(a, b)


def _mm_rope_kernel(a_ref, b_ref, c_ref, sa_ref, sb_ref, o_ref):
    acc = _dot(a_ref[...], b_ref[...])
    c, sa, sb = c_ref[...], sa_ref[...], sb_ref[...]
    for j in range(acc.shape[1] // LANES):
        sl = slice(j * LANES, (j + 1) * LANES)
        o_ref[:, sl] = _rope_lanes(acc[:, sl], c, sa, sb).astype(o_ref.dtype)


def _matmul_rope(a, b, tabs, tm=512):
    m, k = a.shape
    _, n = b.shape
    tab = pl.BlockSpec((tm, LANES), lambda i: (i, 0))
    return pl.pallas_call(
        _mm_rope_kernel, grid=(m // tm,),
        in_specs=[pl.BlockSpec((tm, k), lambda i: (i, 0)),
                  pl.BlockSpec((k, n), lambda i: (0, 0)), tab, tab, tab],
        out_specs=pl.BlockSpec((tm, n), lambda i: (i, 0)),
        out_shape=jax.ShapeDtypeStruct((m, n), BF16),
        compiler_params=_params("parallel"), name="qk_rope_proj")(a, b, *tabs)


def _da_kernel(lam_ref, q_ref, k_ref, v_ref, g_ref, o_ref,
               qlo, qhi, m1, l1, a1, m2, l2, a2, *, nk):
    ki = pl.program_id(3)

    @pl.when(ki == 0)
    def _():
        q = q_ref[...]
        lane = lax.broadcasted_iota(jnp.int32, q.shape, 1)
        zero = jnp.zeros_like(q)
        qlo[...] = jnp.where(lane < DA_HEAD_DIM, q, zero)
        qhi[...] = jnp.where(lane >= DA_HEAD_DIM, q, zero)
        for m, l, a in ((m1, l1, a1), (m2, l2, a2)):
            m[...] = jnp.full(m.shape, -jnp.inf, F32)
            l[...] = jnp.zeros(l.shape, F32)
            a[...] = jnp.zeros(a.shape, F32)

    k = k_ref[...]
    v = v_ref[...]
    for qm, m, l, a in ((qlo, m1, l1, a1), (qhi, m2, l2, a2)):
        s = _dot_nt(qm[...], k)
        m_prev = m[...]
        m_new = jnp.maximum(m_prev, jnp.max(s, -1, keepdims=True))
        alpha = jnp.exp(m_prev - m_new)
        p = jnp.exp(s - m_new)
        l[...] = alpha * l[...] + jnp.sum(p, -1, keepdims=True)
        a[...] = alpha * a[...] + _dot(p.astype(BF16), v)
        m[...] = m_new

    @pl.when(ki == nk - 1)
    def _():
        lam = lam_ref[0]
        out_scale = lam_ref[1]
        o = a1[...] / l1[...] - lam * (a2[...] / l2[...])
        ms = jnp.mean(o * o, -1, keepdims=True)
        o_ref[...] = (o * lax.rsqrt(ms + LN_EPS) * g_ref[...] * out_scale).astype(o_ref.dtype)


def _diff_attention(lam2, qk, v, subln_g, batch, seq, tq=512, tk=512):
    n = qk.shape[0]
    nq, nk = seq // tq, seq // tk
    hd = 2 * DA_HEAD_DIM
    grid = (batch, DA_HEADS, nq, nk)
    return pl.pallas_call(
        functools.partial(_da_kernel, nk=nk), grid=grid,
        in_specs=[pl.BlockSpec(memory_space=pltpu.SMEM),
                  pl.BlockSpec((tq, hd), lambda b, h, qi, ki: (b * nq + qi, h)),
                  pl.BlockSpec((tk, hd), lambda b, h, qi, ki: (b * nk + ki, DA_HEADS + h)),
                  pl.BlockSpec((tk, hd), lambda b, h, qi, ki: (b * nk + ki, h)),
                  pl.BlockSpec((1, hd), lambda b, h, qi, ki: (0, 0))],
        out_specs=pl.BlockSpec((tq, hd), lambda b, h, qi, ki: (b * nq + qi, h)),
        out_shape=jax.ShapeDtypeStruct((n, DA_WIDTH), BF16),
        scratch_shapes=[pltpu.VMEM((tq, hd), BF16), pltpu.VMEM((tq, hd), BF16),
                        pltpu.VMEM((tq, 1), F32), pltpu.VMEM((tq, 1), F32), pltpu.VMEM((tq, hd), F32),
                        pltpu.VMEM((tq, 1), F32), pltpu.VMEM((tq, 1), F32), pltpu.VMEM((tq, hd), F32)],
        compiler_params=_params("parallel", "parallel", "parallel", "arbitrary"),
        name="diff_attention")(lam2, qk, qk, v, subln_g.reshape(1, hd))


def _mla_q_kernel(cq_ref, g_ref, w_ref, c_ref, sa_ref, sb_ref, o_ref, *, scale):
    x = cq_ref[...]
    xn = x * lax.rsqrt(jnp.mean(x * x, -1, keepdims=True) + RMS_EPS) * g_ref[...]
    acc = _dot(xn.astype(BF16), w_ref[...]) * scale
    c, sa, sb = c_ref[...], sa_ref[...], sb_ref[...]
    for h in range(MLA_HEADS):
        base = h * MLA_QK_PAD
        o_ref[:, base:base + LANES] = acc[:, base:base + LANES].astype(o_ref.dtype)
        rope_part = acc[:, base + LANES:base + 2 * LANES]
        o_ref[:, base + LANES:base + 2 * LANES] = _rope_lanes(rope_part, c, sa, sb).astype(o_ref.dtype)


def _mla_q(lat, g, w, tabs, tm=512):
    n = lat.shape[0]
    tab = pl.BlockSpec((tm, LANES), lambda i: (i, 0))
    wd = MLA_HEADS * MLA_QK_PAD
    return pl.pallas_call(
        functools.partial(_mla_q_kernel, scale=(MLA_NOPE + MLA_ROPE) ** -0.5), grid=(n // tm,),
        in_specs=[pl.BlockSpec((tm, MLA_Q_RANK), lambda i: (i, 0)),
                  pl.BlockSpec((1, MLA_Q_RANK), lambda i: (0, 0)),
                  pl.BlockSpec((MLA_Q_RANK, wd), lambda i: (0, 0)), tab, tab, tab],
        out_specs=pl.BlockSpec((tm, wd), lambda i: (i, 0)),
        out_shape=jax.ShapeDtypeStruct((n, wd), BF16),
        compiler_params=_params("parallel"), name="mla_q_up")(lat, g.reshape(1, -1), w, *tabs)


def _mla_kv_kernel(ckv_ref, kr_ref, g_ref, wk_ref, wv_ref, c_ref, sa_ref, sb_ref, k_ref, v_ref):
    x = ckv_ref[...]
    xn = (x * lax.rsqrt(jnp.mean(x * x, -1, keepdims=True) + RMS_EPS) * g_ref[...]).astype(BF16)
    kn = _dot(xn, wk_ref[...])
    v_ref[...] = _dot(xn, wv_ref[...]).astype(v_ref.dtype)
    kr = _rope_lanes(kr_ref[...], c_ref[...], sa_ref[...], sb_ref[...]).astype(k_ref.dtype)
    for h in range(MLA_HEADS):
        base = h * MLA_QK_PAD
        k_ref[:, base:base + LANES] = kn[:, h * MLA_NOPE:(h + 1) * MLA_NOPE].astype(k_ref.dtype)
        k_ref[:, base + LANES:base + 2 * LANES] = kr


def _mla_kv(lat, g, wk, wv, tabs, tm=512):
    n = lat.shape[0]
    tab = pl.BlockSpec((tm, LANES), lambda i: (i, 0))
    wd = MLA_HEADS * MLA_QK_PAD
    ckv_blk = MLA_Q_RANK // MLA_KV_RANK
    kr_blk = (MLA_Q_RANK + MLA_KV_RANK) // LANES
    return pl.pallas_call(
        _mla_kv_kernel, grid=(n // tm,),
        in_specs=[pl.BlockSpec((tm, MLA_KV_RANK), lambda i: (i, ckv_blk)),
                  pl.BlockSpec((tm, LANES), lambda i: (i, kr_blk)),
                  pl.BlockSpec((1, MLA_KV_RANK), lambda i: (0, 0)),
                  pl.BlockSpec((MLA_KV_RANK, MLA_WIDTH), lambda i: (0, 0)),
                  pl.BlockSpec((MLA_KV_RANK, MLA_WIDTH), lambda i: (0, 0)), tab, tab, tab],
        out_specs=[pl.BlockSpec((tm, wd), lambda i: (i, 0)),
                   pl.BlockSpec((tm, MLA_WIDTH), lambda i: (i, 0))],
        out_shape=[jax.ShapeDtypeStruct((n, wd), BF16), jax.ShapeDtypeStruct((n, MLA_WIDTH), BF16)],
        compiler_params=_params("parallel"), name="mla_kv_up")(lat, lat, g.reshape(1, -1), wk, wv, *tabs)


def _mla_attn_kernel(q_ref, k_ref, v_ref, o_ref, m, l, a, *, nk):
    ki = pl.program_id(3)

    @pl.when(ki == 0)
    def _():
        m[...] = jnp.full(m.shape, -jnp.inf, F32)
        l[...] = jnp.zeros(l.shape, F32)
        a[...] = jnp.zeros(a.shape, F32)

    s = _dot_nt(q_ref[...], k_ref[...])
    m_prev = m[...]
    m_new = jnp.maximum(m_prev, jnp.max(s, -1, keepdims=True))
    alpha = jnp.exp(m_prev - m_new)
    p = jnp.exp(s - m_new)
    l[...] = alpha * l[...] + jnp.sum(p, -1, keepdims=True)
    a[...] = alpha * a[...] + _dot(p.astype(BF16), v_ref[...])
    m[...] = m_new

    @pl.when(ki == nk - 1)
    def _():
        o_ref[...] = (a[...] / l[...]).astype(o_ref.dtype)


def _mla_attention(q, k, v, batch, seq, tq=512, tk=512):
    n = q.shape[0]
    nq, nk = seq // tq, seq // tk
    return pl.pallas_call(
        functools.partial(_mla_attn_kernel, nk=nk), grid=(batch, MLA_HEADS, nq, nk),
        in_specs=[pl.BlockSpec((tq, MLA_QK_PAD), lambda b, h, qi, ki: (b * nq + qi, h)),
                  pl.BlockSpec((tk, MLA_QK_PAD), lambda b, h, qi, ki: (b * nk + ki, h)),
                  pl.BlockSpec((tk, MLA_V), lambda b, h, qi, ki: (b * nk + ki, h))],
        out_specs=pl.BlockSpec((tq, MLA_V), lambda b, h, qi, ki: (b * nq + qi, h)),
        out_shape=jax.ShapeDtypeStruct((n, MLA_WIDTH), BF16),
        scratch_shapes=[pltpu.VMEM((tq, 1), F32), pltpu.VMEM((tq, 1), F32), pltpu.VMEM((tq, MLA_V), F32)],
        compiler_params=_params("parallel", "parallel", "parallel", "arbitrary"),
        name="mla_attention")(q, k, v)


def _short_conv_kernel(u_ref, w_ref, b_ref, o_ref):
    u = u_ref[...]
    seq = u.shape[0]
    row = lax.broadcasted_iota(jnp.int32, u.shape, 0)
    prev = jnp.where(row == 0, 0.0, pltpu.roll(u, 1, 0))
    nxt = jnp.where(row == seq - 1, 0.0, pltpu.roll(u, seq - 1, 0))
    w = w_ref[...]
    o_ref[...] = prev * w[0:1] + u * w[1:2] + nxt * w[2:3] + b_ref[...]


def _short_conv(u, w, b, batch, seq, tc=256):
    n, c = u.shape
    return pl.pallas_call(
        _short_conv_kernel, grid=(batch, c // tc),
        in_specs=[pl.BlockSpec((seq, tc), lambda bi, j: (bi, j)),
                  pl.BlockSpec((3, tc), lambda bi, j: (0, j)),
                  pl.BlockSpec((1, tc), lambda bi, j: (0, j))],
        out_specs=pl.BlockSpec((seq, tc), lambda bi, j: (bi, j)),
        out_shape=jax.ShapeDtypeStruct((n, c), F32),
        compiler_params=_params("parallel", "parallel"), name="hyena_short_conv")(u, w, b.reshape(1, c))


def _filter_hidden_kernel(z_ref, w1_ref, b1_ref, w2_ref, b2_ref, w3_ref, b3_ref, fr_ref, o_ref):
    hp = lax.Precision.HIGHEST
    fr = fr_ref[...]
    h = jnp.sin(fr * (jnp.dot(z_ref[...], w1_ref[...], precision=hp, preferred_element_type=F32) + b1_ref[...]))
    h = jnp.sin(fr * (jnp.dot(h, w2_ref[...], precision=hp, preferred_element_type=F32) + b2_ref[...]))
    o_ref[...] = jnp.sin(fr * (jnp.dot(h, w3_ref[...], precision=hp, preferred_element_type=F32) + b3_ref[...]))


def _filter_hidden(z_pad, w1, b1, w2, b2, w3, b3, freq):
    seq = z_pad.shape[0]

    def padm(w):
        return jnp.zeros((LANES, LANES), F32).at[:w.shape[0], :w.shape[1]].set(w)

    def padv(v):
        return jnp.zeros((1, LANES), F32).at[0, :v.shape[0]].set(v)

    return pl.pallas_call(
        _filter_hidden_kernel,
        out_shape=jax.ShapeDtypeStruct((seq, LANES), F32),
        compiler_params=pltpu.CompilerParams(vmem_limit_bytes=VMEM_LIMIT_BYTES),
        name="hyena_filter_hidden")(z_pad, padm(w1), padv(b1), padm(w2), padv(b2), padm(w3), padv(b3), padv(freq))


def _filter_pm_kernel(hid_ref, wf_ref, wb_ref, dec_ref, p_ref, m_ref):
    hp = lax.Precision.HIGHEST
    hid = hid_ref[...]
    dec = dec_ref[...]
    fwd = jnp.dot(hid, wf_ref[...], precision=hp, preferred_element_type=F32) * dec
    bwd = jnp.dot(hid, wb_ref[...], precision=hp, preferred_element_type=F32) * dec
    first = lax.broadcasted_iota(jnp.int32, fwd.shape, 0) == 0
    a = jnp.where(first, fwd + bwd, fwd)
    bt = jnp.where(first, 0.0, bwd)
    norm = jnp.sum(jnp.abs(a), 0, keepdims=True) + jnp.sum(jnp.abs(bt), 0, keepdims=True)
    inv = 1.0 / norm
    p_ref[...] = ((a + bt) * inv).astype(p_ref.dtype)
    m_ref[...] = ((a - bt) * inv).astype(m_ref.dtype)


def _filter_pm(hid, wout_pad, decay, tc=256):
    seq = hid.shape[0]
    cb = HY_WIDTH // tc
    out = jax.ShapeDtypeStruct((seq, HY_ORDER * HY_WIDTH), BF16)
    oblk = pl.BlockSpec((seq, tc), lambda n, j: (0, n * cb + j))
    return pl.pallas_call(
        _filter_pm_kernel, grid=(HY_ORDER, cb),
        in_specs=[pl.BlockSpec((seq, LANES), lambda n, j: (0, 0)),
                  pl.BlockSpec((LANES, tc), lambda n, j: (0, 2 * n * cb + j)),
                  pl.BlockSpec((LANES, tc), lambda n, j: (0, (2 * n + 1) * cb + j)),
                  pl.BlockSpec((seq, tc), lambda n, j: (0, j))],
        out_specs=[oblk, oblk], out_shape=[out, out],
        compiler_params=_params("parallel", "parallel"), name="hyena_filter_pm")(hid, wout_pad, wout_pad, decay)


def _dft_fwd_mul_kernel(fc_ref, fs_ref, z_ref, p_ref, q_ref, w1_ref, w2_ref):
    z = z_ref[...].astype(BF16)
    cu = _dot(fc_ref[...], z)
    su = _dot(fs_ref[...], z)
    p, q = p_ref[...], q_ref[...]
    w1_ref[0] = (cu * p - su * q).astype(w1_ref.dtype)
    w2_ref[0] = (cu * q + su * p).astype(w2_ref.dtype)


def _dft_fwd_mul(fc, fs, z, z_col_blk, pspec, qspec, order, batch, seq, tf=512, tc=256):
    cb = HY_WIDTH // tc
    out = jax.ShapeDtypeStruct((batch, seq, HY_WIDTH), BF16)
    oblk = pl.BlockSpec((1, tf, tc), lambda fi, b, j: (b, fi, j))
    kblk = pl.BlockSpec((tf, tc), lambda fi, b, j: (fi, order * cb + j))
    fblk = pl.BlockSpec((tf, seq), lambda fi, b, j: (fi, 0))
    return pl.pallas_call(
        _dft_fwd_mul_kernel, grid=(seq // tf, batch, cb),
        in_specs=[fblk, fblk,
                  pl.BlockSpec((seq, tc), lambda fi, b, j: (b, z_col_blk * cb + j)),
                  kblk, kblk],
        out_specs=[oblk, oblk], out_shape=[out, out],
        compiler_params=_params("parallel", "parallel", "parallel"),
        name="hyena_dft_fwd")(fc, fs, z, pspec, qspec)


def _dft_inv_gate_kernel(fct_ref, fst_ref, w1_ref, w2_ref, z_ref, gate_ref, skip_ref, o_ref, *, inv_len):
    y = (_dot(fct_ref[...], w1_ref[0]) + _dot(fst_ref[...], w2_ref[0])) * inv_len
    o_ref[...] = (gate_ref[...] * (y + z_ref[...] * skip_ref[...])).astype(o_ref.dtype)


def _dft_inv_gate(fct, fst, w1, w2, z, z_col_blk, gate, gate_col_blk, skip, out_dtype,
                  batch, seq, tt=512, tc=256):
    cb = HY_WIDTH // tc
    nt = seq // tt
    fblk = pl.BlockSpec((tt, seq), lambda ti, b, j: (ti, 0))
    wblk = pl.BlockSpec((1, seq, tc), lambda ti, b, j: (b, 0, j))
    return pl.pallas_call(
        functools.partial(_dft_inv_gate_kernel, inv_len=1.0 / seq), grid=(nt, batch, cb),
        in_specs=[fblk, fblk, wblk, wblk,
                  pl.BlockSpec((tt, tc), lambda ti, b, j: (b * nt + ti, z_col_blk * cb + j)),
                  pl.BlockSpec((tt, tc), lambda ti, b, j: (b * nt + ti, gate_col_blk * cb + j)),
                  pl.BlockSpec((1, tc), lambda ti, b, j: (0, j))],
        out_specs=pl.BlockSpec((tt, tc), lambda ti, b, j: (b * nt + ti, j)),
        out_shape=jax.ShapeDtypeStruct((batch * seq, HY_WIDTH), out_dtype),
        compiler_params=_params("parallel", "parallel", "parallel"),
        name="hyena_dft_inv")(fct, fst, w1, w2, z, gate, skip.reshape(1, HY_WIDTH))


def _merge_kernel(h_ref, wga_ref, wgb_ref, wgc_ref, gb_ref, oa_ref, ob_ref, oc_ref,
                  wa_ref, wb_ref, wc_ref, o_ref):
    hb = h_ref[...]
    acc = None
    for i, (wg, o, w) in enumerate(((wga_ref, oa_ref, wa_ref), (wgb_ref, ob_ref, wb_ref),
                                    (wgc_ref, oc_ref, wc_ref))):
        g = _dot(hb, wg[...]) + gb_ref[i]
        term = (1.0 / (1.0 + jnp.exp(-g))) * _dot(o[...], w[...])
        acc = term if acc is None else acc + term
    o_ref[...] = acc.astype(o_ref.dtype)


def _merge(hb, wgs, gate_b, outs, wbs, tm=512, tn=512):
    n, d = hb.shape
    row = lambda w: pl.BlockSpec((tm, w), lambda i, j: (i, 0))
    col = lambda k: pl.BlockSpec((k, tn), lambda i, j: (0, j))
    return pl.pallas_call(
        _merge_kernel, grid=(n // tm, d // tn),
        in_specs=[row(d), col(d), col(d), col(d),
                  pl.BlockSpec((N_BRANCH, 1, tn), lambda i, j: (0, 0, j)),
                  row(DA_WIDTH), row(HY_WIDTH), row(MLA_WIDTH),
                  col(DA_WIDTH), col(HY_WIDTH), col(MLA_WIDTH)],
        out_specs=pl.BlockSpec((tm, tn), lambda i, j: (i, j)),
        out_shape=jax.ShapeDtypeStruct((n, d), BF16),
        compiler_params=_params("parallel", "parallel"),
        name="gated_merge")(hb, *wgs, gate_b.reshape(N_BRANCH, 1, d), *outs, *wbs)


def _mm_res_ln_kernel(a_ref, b_ref, h_ref, g_ref, beta_ref, o_ref, ob_ref, acc_ref, *, nk):
    ki = pl.program_id(1)

    @pl.when(ki == 0)
    def _():
        acc_ref[...] = jnp.zeros(acc_ref.shape, F32)

    acc_ref[...] += _dot(a_ref[...], b_ref[...])

    @pl.when(ki == nk - 1)
    def _():
        y = _layer_norm(DEEPNORM_ALPHA * h_ref[...] + acc_ref[...], g_ref[...], beta_ref[...])
        o_ref[...] = y
        ob_ref[...] = y.astype(BF16)


def _matmul_res_ln(a, b, h, g, beta, tm=512, tk=1024, name="matmul_res_ln"):
    n, k = a.shape
    d = b.shape[1]
    nk = k // tk
    row = pl.BlockSpec((tm, d), lambda i, kk: (i, 0))
    vec = pl.BlockSpec((1, d), lambda i, kk: (0, 0))
    return pl.pallas_call(
        functools.partial(_mm_res_ln_kernel, nk=nk), grid=(n // tm, nk),
        in_specs=[pl.BlockSpec((tm, tk), lambda i, kk: (i, kk)),
                  pl.BlockSpec((tk, d), lambda i, kk: (kk, 0)), row, vec, vec],
        out_specs=[row, row],
        out_shape=[jax.ShapeDtypeStruct((n, d), F32), jax.ShapeDtypeStruct((n, d), BF16)],
        scratch_shapes=[pltpu.VMEM((tm, d), F32)],
        compiler_params=_params("parallel", "arbitrary"), name=name)(a, b, h, g.reshape(1, d), beta.reshape(1, d))


def _rope_tables(positions):
    inv = ROPE_THETA ** (-jnp.arange(0, ROPE_DIM, 2, dtype=F32) / ROPE_DIM)
    ang = positions.astype(F32).reshape(-1, 1) * inv
    cos, sin = jnp.cos(ang), jnp.sin(ang)
    zero = jnp.zeros_like(sin)
    c = jnp.concatenate([cos, cos, cos, cos], -1)
    sa = jnp.concatenate([-sin, zero, -sin, zero], -1)
    sb = jnp.concatenate([zero, sin, zero, sin], -1)
    return c, sa, sb


def _hyena_features(seq_len):
    t = jnp.linspace(0.0, 1.0, seq_len, dtype=F32)[:, None]
    bands = (HY_EMB - 1) // 2
    w = 2.0 * math.pi * jnp.arange(seq_len, dtype=F32)[:, None] / seq_len
    f = jnp.linspace(1e-4, bands - 1, bands, dtype=F32)[None, :]
    ang = f * w
    z = jnp.concatenate([t, jnp.cos(ang), -jnp.sin(ang)], axis=-1)
    deltas = jnp.linspace(math.log(HY_DECAY_TARGET) / HY_FAST_DECAY,
                          math.log(HY_DECAY_TARGET) / HY_SLOW_DECAY, HY_WIDTH, dtype=F32)
    decay = jnp.exp(-t * jnp.abs(deltas)[None, :])
    z_pad = jnp.zeros((seq_len, LANES), F32).at[:, :HY_EMB].set(z)
    return z_pad, decay


def _dft_tables(seq_len):
    f = jnp.arange(seq_len, dtype=jnp.int32)[:, None]
    s = jnp.arange(seq_len, dtype=jnp.int32)[None, :]
    idx = ((2 * f + 1) * s) % (4 * seq_len)
    ang = idx.astype(F32) * (2.0 * math.pi / (4 * seq_len))
    fc = jnp.cos(ang).astype(BF16)
    fs = jnp.sin(ang).astype(BF16)
    return fc, fs, fc.T, fs.T


def kernel(x, positions, ln_emb_g, ln_emb_b, w_in, gate_b, da_lambda, da_subln_g, hy_conv_w, hy_conv_b, hy_f_w1, hy_f_b1, hy_f_w2, hy_f_b2, hy_f_w3, hy_f_b3, hy_f_freq, hy_f_wout, hy_skip, mla_q_norm_g, mla_kv_norm_g, mla_w_uq, mla_w_ukv, w_branch_a, w_branch_b, w_branch_c, w_out, ln1_g, ln1_b, mlp_w1, mlp_w2, ln2_g, ln2_b):
    batch, seq, d = x.shape
    n = batch * seq
    tabs = _rope_tables(positions)
    z_pad, decay = _hyena_features(seq)
    fc, fs, fct, fst = _dft_tables(seq)

    h, hb = _entry_ln(x.reshape(n, d), ln_emb_g, ln_emb_b)

    o_qk = 0
    o_v = 2 * DA_WIDTH
    o_hy = 3 * DA_WIDTH
    o_lat = o_hy + 3 * HY_WIDTH
    lat_w = MLA_Q_RANK + MLA_KV_RANK + MLA_ROPE
    o_g = o_lat + lat_w

    for l in range(DEPTH):
        lambda_init = 0.8 - 0.6 * math.exp(-0.3 * l)
        wl = w_in[l]
        da_scale = DA_HEAD_DIM ** -0.5
        w_qk = jnp.concatenate([wl[:, :DA_WIDTH] * da_scale, wl[:, DA_WIDTH:2 * DA_WIDTH]], 1).astype(BF16)
        w_va = wl[:, o_v:o_v + DA_WIDTH].astype(BF16)
        w_hy = wl[:, o_hy:o_hy + 3 * HY_WIDTH].astype(BF16)
        w_lat = jnp.pad(wl[:, o_lat:o_lat + lat_w], ((0, 0), (0, LANES - MLA_ROPE))).astype(BF16)
        wgs = [wl[:, o_g + i * d:o_g + (i + 1) * d].astype(BF16) for i in range(N_BRANCH)]

        qk = _matmul_rope(hb, w_qk, tabs)
        va = _matmul(hb, w_va, BF16, 512, DA_WIDTH, name="da_v_proj")
        lq1, lk1, lq2, lk2 = da_lambda[l].astype(F32)
        lam = jnp.exp(jnp.sum(lq1 * lk1)) - jnp.exp(jnp.sum(lq2 * lk2)) + lambda_init
        lam2 = jnp.stack([lam, jnp.asarray(1.0 - lambda_init, F32)]).astype(F32)
        o_a = _diff_attention(lam2, qk, va, da_subln_g[l], batch, seq)

        u_hy = _matmul(hb, w_hy, F32, 512, 1024, name="hyena_in_proj")
        uc = _short_conv(u_hy, hy_conv_w[l], hy_conv_b[l], batch, seq)
        hid = _filter_hidden(z_pad, hy_f_w1[l], hy_f_b1[l], hy_f_w2[l], hy_f_b2[l],
                             hy_f_w3[l], hy_f_b3[l], hy_f_freq[l])
        wout_pad = jnp.zeros((LANES, 2 * HY_ORDER * HY_WIDTH), F32).at[:HY_FILTER_WIDTH].set(hy_f_wout[l])
        p_filt, m_filt = _filter_pm(hid, wout_pad, decay)
        pspec = _matmul(fc, p_filt, F32, 512, 1024, name="hyena_filter_dft_c")
        qspec = _matmul(fs, m_filt, F32, 512, 1024, name="hyena_filter_dft_s")
        w1, w2 = _dft_fwd_mul(fc, fs, uc, 0, pspec, qspec, 0, batch, seq)
        z1 = _dft_inv_gate(fct, fst, w1, w2, uc, 0, uc, 1, hy_skip[l, 0], F32, batch, seq)
        w1, w2 = _dft_fwd_mul(fc, fs, z1, 0, pspec, qspec, 1, batch, seq)
        o_b = _dft_inv_gate(fct, fst, w1, w2, z1, 0, uc, 2, hy_skip[l, 1], BF16, batch, seq)

        lat = _matmul(hb, w_lat, F32, 512, w_lat.shape[1], name="mla_latent_proj")
        wq = mla_w_uq[l].reshape(MLA_Q_RANK, MLA_HEADS, MLA_NOPE + MLA_ROPE)
        wq = jnp.pad(wq, ((0, 0), (0, 0), (0, MLA_QK_PAD - MLA_NOPE - MLA_ROPE)))
        wq = wq.reshape(MLA_Q_RANK, MLA_HEADS * MLA_QK_PAD).astype(BF16)
        wkv = mla_w_ukv[l].reshape(MLA_KV_RANK, MLA_HEADS, MLA_NOPE + MLA_V)
        wk = wkv[:, :, :MLA_NOPE].reshape(MLA_KV_RANK, MLA_WIDTH).astype(BF16)
        wv = wkv[:, :, MLA_NOPE:].reshape(MLA_KV_RANK, MLA_WIDTH).astype(BF16)
        q_cat = _mla_q(lat, mla_q_norm_g[l], wq, tabs)
        k_cat, v_c = _mla_kv(lat, mla_kv_norm_g[l], wk, wv, tabs)
        o_c = _mla_attention(q_cat, k_cat, v_c, batch, seq)

        merged = _merge(hb, wgs, gate_b[l], (o_a, o_b, o_c),
                        (w_branch_a[l].astype(BF16), w_branch_b[l].astype(BF16), w_branch_c[l].astype(BF16)))
        h, hb = _matmul_res_ln(merged, w_out[l].astype(BF16), h, ln1_g[l], ln1_b[l], tk=d, name="out_proj_ln")

        ff = _matmul(hb, mlp_w1[l].astype(BF16), BF16, 1024, 1024, square_relu=True, name="mlp_up")
        h, hb = _matmul_res_ln(ff, mlp_w2[l].astype(BF16), h, ln2_g[l], ln2_b[l], name="mlp_down_ln")

    return h.reshape(batch, seq, d)
```

```python
import functools
import math

import jax
import jax.numpy as jnp
from jax import lax
from jax.experimental import pallas as pl
from jax.experimental.pallas import tpu as pltpu

D_MODEL = 2048
DEPTH = 4
ROPE_DIM = 64
ROPE_THETA = 10000.0
DA_HEADS = 8
DA_HEAD_DIM = 64
DA_WIDTH = DA_HEADS * 2 * DA_HEAD_DIM
HY_WIDTH = D_MODEL // 2
HY_ORDER = 2
HY_EMB = 33
HY_FILTER_WIDTH = 64
HY_DECAY_TARGET = 1e-2
HY_FAST_DECAY = 0.3
HY_SLOW_DECAY = 1.5
MLA_HEADS = 8
MLA_NOPE = 128
MLA_ROPE = 64
MLA_V = 128
MLA_Q_RANK = D_MODEL // 4
MLA_KV_RANK = D_MODEL // 8
MLA_WIDTH = MLA_HEADS * MLA_V
N_BRANCH = 3
D_FF = 4 * D_MODEL
DEEPNORM_ALPHA = (2 * DEPTH) ** 0.25
LN_EPS = 1e-5
RMS_EPS = 1e-6

LANES = 128
BF16_SUBLANES = 16
MLA_QK_PAD = 2 * LANES
LOG2_E = math.log2(math.e)
VMEM_LIMIT_BYTES = 56 * 1024 * 1024

BF16 = jnp.bfloat16
F32 = jnp.float32


def _params(*semantics):
    return pltpu.CompilerParams(dimension_semantics=semantics,
                                vmem_limit_bytes=VMEM_LIMIT_BYTES)


def _dot(a, b):
    return jnp.dot(a, b, preferred_element_type=F32)


def _dot_nt(a, b):
    return lax.dot_general(a, b, (((1,), (1,)), ((), ())), preferred_element_type=F32)


def _layer_norm(v, g, b):
    mu = jnp.mean(v, -1, keepdims=True)
    d = v - mu
    var = jnp.mean(d * d, -1, keepdims=True)
    return d * lax.rsqrt(var + LN_EPS) * g + b


def _rope_lanes(x, c, sa, sb):
    return x * c + pltpu.roll(x, LANES - 32, 1) * sa + pltpu.roll(x, 32, 1) * sb


def _ln_kernel(x_ref, g_ref, b_ref, o_ref, ob_ref):
    y = _layer_norm(x_ref[...], g_ref[...], b_ref[...])
    o_ref[...] = y
    ob_ref[...] = y.astype(BF16)


def _entry_ln(x, g, b, tm=512):
    n, d = x.shape
    row = pl.BlockSpec((tm, d), lambda i: (i, 0))
    vec = pl.BlockSpec((1, d), lambda i: (0, 0))
    return pl.pallas_call(
        _ln_kernel, grid=(n // tm,), in_specs=[row, vec, vec], out_specs=[row, row],
        out_shape=[jax.ShapeDtypeStruct((n, d), F32), jax.ShapeDtypeStruct((n, d), BF16)],
        compiler_params=_params("parallel"), name="entry_ln")(x, g.reshape(1, d), b.reshape(1, d))


def _mm_kernel(a_ref, b_ref, o_ref, *, square_relu):
    acc = _dot(a_ref[...], b_ref[...])
    if square_relu:
        acc = jnp.square(jnp.maximum(acc, 0.0))
    o_ref[...] = acc.astype(o_ref.dtype)


def _matmul(a, b, out_dtype, tm, tn, square_relu=False, name="matmul"):
    m, k = a.shape
    _, n = b.shape
    return pl.pallas_call(
        functools.partial(_mm_kernel, square_relu=square_relu),
        grid=(m // tm, n // tn),
        in_specs=[pl.BlockSpec((tm, k), lambda i, j: (i, 0)),
                  pl.BlockSpec((k, tn), lambda i, j: (0, j))],
        out_specs=pl.BlockSpec((tm, tn), lambda i, j: (i, j)),
        out_shape=jax.ShapeDtypeStruct((m, n), out_dtype),
        compiler_params=_params("parallel", "parallel"), name=name)(a, b)


def _mm_rope_kernel(a_ref, b_ref, c_ref, sa_ref, sb_ref, o_ref):
    acc = _dot(a_ref[...], b_ref[...])
    c, sa, sb = c_ref[...], sa_ref[...], sb_ref[...]
    for j in range(acc.shape[1] // LANES):
        sl = slice(j * LANES, (j + 1) * LANES)
        o_ref[:, sl] = _rope_lanes(acc[:, sl], c, sa, sb).astype(o_ref.dtype)


def _matmul_rope(a, b, tabs, tm=512):
    m, k = a.shape
    _, n = b.shape
    tab = pl.BlockSpec((tm, LANES), lambda i: (i, 0))
    return pl.pallas_call(
        _mm_rope_kernel, grid=(m // tm,),
        in_specs=[pl.BlockSpec((tm, k), lambda i: (i, 0)),
                  pl.BlockSpec((k, n), lambda i: (0, 0)), tab, tab, tab],
        out_specs=pl.BlockSpec((tm, n), lambda i: (i, 0)),
        out_shape=jax.ShapeDtypeStruct((m, n), BF16),
        compiler_params=_params("parallel"), name="qk_rope_proj")(a, b, *tabs)


def _rope_rows(x1, x2, cos_t, sin_t):
    return x1 * cos_t - x2 * sin_t, x2 * cos_t + x1 * sin_t


def _proj_t_kernel(wt_ref, h_ref, cos_ref, sin_ref, o_ref, *, rope_rows):
    acc = _dot_nt(wt_ref[...], h_ref[...])
    cos_t, sin_t = cos_ref[...], sin_ref[...]
    half = ROPE_DIM // 2
    for g in range(rope_rows // ROPE_DIM):
        r = g * ROPE_DIM
        y1, y2 = _rope_rows(acc[r:r + half], acc[r + half:r + ROPE_DIM], cos_t, sin_t)
        o_ref[r:r + half, :] = y1.astype(o_ref.dtype)
        o_ref[r + half:r + ROPE_DIM, :] = y2.astype(o_ref.dtype)
    o_ref[rope_rows:, :] = acc[rope_rows:].astype(o_ref.dtype)


def _proj_t(wt, hb, tabs_t, rope_rows, tm=512):
    f, k = wt.shape
    n = hb.shape[0]
    tab = pl.BlockSpec((ROPE_DIM // 2, tm), lambda i: (0, i))
    return pl.pallas_call(
        functools.partial(_proj_t_kernel, rope_rows=rope_rows), grid=(n // tm,),
        in_specs=[pl.BlockSpec((f, k), lambda i: (0, 0)),
                  pl.BlockSpec((tm, k), lambda i: (i, 0)), tab, tab],
        out_specs=pl.BlockSpec((f, tm), lambda i: (0, i)),
        out_shape=jax.ShapeDtypeStruct((f, n), BF16),
        compiler_params=_params("parallel"), name="da_qv_proj_t")(wt, hb, *tabs_t)


def _attn_t_kernel(lam_ref, qt_ref, k_ref, vt_ref, g_ref, o_ref, *, n_maps, tk, nk):
    qt = qt_ref[...]
    tq = qt.shape[1]
    if n_maps == 2:
        row = lax.broadcasted_iota(jnp.int32, qt.shape, 0)
        zero = jnp.zeros_like(qt)
        split = qt.shape[0] // 2
        qcat = jnp.concatenate([jnp.where(row < split, qt, zero), jnp.where(row >= split, qt, zero)], axis=1)
    else:
        qcat = qt
    w = qcat.shape[1]
    dv = vt_ref.shape[0]
    ones = jnp.ones((BF16_SUBLANES, tk), BF16)
    m = jnp.full((1, w), -jnp.inf, F32)
    acc = jnp.zeros((dv + BF16_SUBLANES, w), F32)
    for j in range(nk):
        kj = k_ref[j * tk:(j + 1) * tk, :]
        vj = jnp.concatenate([vt_ref[:, j * tk:(j + 1) * tk], ones], axis=0)
        s = _dot(kj, qcat)
        m_new = jnp.maximum(m, jnp.max(s, axis=0, keepdims=True))
        alpha = jnp.exp2(m - m_new)
        p = jnp.exp2(s - m_new).astype(BF16)
        acc = acc * alpha + _dot(vj, p)
        m = m_new
    o = acc[:dv] / acc[dv:dv + 1]
    if n_maps == 2:
        o = o[:, :tq] - lam_ref[0] * o[:, tq:]
        ms = jnp.mean(o * o, axis=0, keepdims=True)
        o = o * lax.rsqrt(ms + LN_EPS) * g_ref[...] * lam_ref[1]
    o_ref[...] = o.T.astype(o_ref.dtype)


def _attention_t(lam2, qt, k, vt, v_row_blk, g, n_maps, batch, seq, heads, dqk, dv, tq, tk, name):
    nq, nk = seq // tq, seq // tk
    return pl.pallas_call(
        functools.partial(_attn_t_kernel, n_maps=n_maps, tk=tk, nk=nk),
        grid=(batch, heads, nq),
        in_specs=[pl.BlockSpec(memory_space=pltpu.SMEM),
                  pl.BlockSpec((dqk, tq), lambda b, h, qi: (h, b * nq + qi)),
                  pl.BlockSpec((seq, dqk), lambda b, h, qi: (b, h)),
                  pl.BlockSpec((dv, seq), lambda b, h, qi: (v_row_blk + h, b)),
                  pl.BlockSpec((dv, 1), lambda b, h, qi: (0, 0))],
        out_specs=pl.BlockSpec((tq, dv), lambda b, h, qi: (b * nq + qi, h)),
        out_shape=jax.ShapeDtypeStruct((batch * seq, heads * dv), BF16),
        compiler_params=_params("parallel", "parallel", "parallel"),
        name=name)(lam2, qt, k, vt, g)


def _rms_norm(x, g):
    return x * lax.rsqrt(jnp.mean(x * x, -1, keepdims=True) + RMS_EPS) * g


def _mla_q_t_kernel(cq_ref, g_ref, wt_ref, cos_ref, sin_ref, o_ref, *, scale):
    xn = _rms_norm(cq_ref[...], g_ref[...]).astype(BF16)
    acc = _dot_nt(wt_ref[...], xn) * scale
    cos_t, sin_t = cos_ref[...], sin_ref[...]
    half = MLA_ROPE // 2
    for h in range(MLA_HEADS):
        base = h * MLA_QK_PAD
        r = base + MLA_NOPE
        o_ref[base:r, :] = acc[base:r].astype(o_ref.dtype)
        y1, y2 = _rope_rows(acc[r:r + half], acc[r + half:r + MLA_ROPE], cos_t, sin_t)
        o_ref[r:r + half, :] = y1.astype(o_ref.dtype)
        o_ref[r + half:r + MLA_ROPE, :] = y2.astype(o_ref.dtype)
        o_ref[r + MLA_ROPE:base + MLA_QK_PAD, :] = acc[r + MLA_ROPE:base + MLA_QK_PAD].astype(o_ref.dtype)


def _mla_q_t(lat, g, wt, tabs_t, scale, tm=512):
    n = lat.shape[0]
    tab = pl.BlockSpec((MLA_ROPE // 2, tm), lambda i: (0, i))
    wd = MLA_HEADS * MLA_QK_PAD
    return pl.pallas_call(
        functools.partial(_mla_q_t_kernel, scale=scale), grid=(n // tm,),
        in_specs=[pl.BlockSpec((tm, MLA_Q_RANK), lambda i: (i, 0)),
                  pl.BlockSpec((1, MLA_Q_RANK), lambda i: (0, 0)),
                  pl.BlockSpec((wd, MLA_Q_RANK), lambda i: (0, 0)), tab, tab],
        out_specs=pl.BlockSpec((wd, tm), lambda i: (0, i)),
        out_shape=jax.ShapeDtypeStruct((wd, n), BF16),
        compiler_params=_params("parallel"), name="mla_q_up_t")(lat, g.reshape(1, -1), wt, *tabs_t)


def _mla_kv_kernel(ckv_ref, kr_ref, g_ref, wk_ref, wvt_ref, c_ref, sa_ref, sb_ref, k_ref, vt_ref):
    xn = _rms_norm(ckv_ref[...], g_ref[...]).astype(BF16)
    kn = _dot(xn, wk_ref[...])
    vt_ref[...] = _dot_nt(wvt_ref[...], xn).astype(vt_ref.dtype)
    kr = _rope_lanes(kr_ref[...], c_ref[...], sa_ref[...], sb_ref[...]).astype(k_ref.dtype)
    for h in range(MLA_HEADS):
        base = h * MLA_QK_PAD
        k_ref[:, base:base + LANES] = kn[:, h * MLA_NOPE:(h + 1) * MLA_NOPE].astype(k_ref.dtype)
        k_ref[:, base + LANES:base + 2 * LANES] = kr


def _mla_kv(lat, g, wk, wvt, tabs, tm=512):
    n = lat.shape[0]
    tab = pl.BlockSpec((tm, LANES), lambda i: (i, 0))
    wd = MLA_HEADS * MLA_QK_PAD
    ckv_blk = MLA_Q_RANK // MLA_KV_RANK
    kr_blk = (MLA_Q_RANK + MLA_KV_RANK) // LANES
    return pl.pallas_call(
        _mla_kv_kernel, grid=(n // tm,),
        in_specs=[pl.BlockSpec((tm, MLA_KV_RANK), lambda i: (i, ckv_blk)),
                  pl.BlockSpec((tm, LANES), lambda i: (i, kr_blk)),
                  pl.BlockSpec((1, MLA_KV_RANK), lambda i: (0, 0)),
                  pl.BlockSpec((MLA_KV_RANK, MLA_WIDTH), lambda i: (0, 0)),
                  pl.BlockSpec((MLA_WIDTH, MLA_KV_RANK), lambda i: (0, 0)), tab, tab, tab],
        out_specs=[pl.BlockSpec((tm, wd), lambda i: (i, 0)),
                   pl.BlockSpec((MLA_WIDTH, tm), lambda i: (0, i))],
        out_shape=[jax.ShapeDtypeStruct((n, wd), BF16), jax.ShapeDtypeStruct((MLA_WIDTH, n), BF16)],
        compiler_params=_params("parallel"), name="mla_kv_up")(lat, lat, g.reshape(1, -1), wk, wvt, *tabs)


def _short_conv_kernel(u_ref, w_ref, b_ref, o_ref):
    u = u_ref[...]
    seq = u.shape[0]
    row = lax.broadcasted_iota(jnp.int32, u.shape, 0)
    prev = jnp.where(row == 0, 0.0, pltpu.roll(u, 1, 0))
    nxt = jnp.where(row == seq - 1, 0.0, pltpu.roll(u, seq - 1, 0))
    w = w_ref[...]
    o_ref[...] = prev * w[0:1] + u * w[1:2] + nxt * w[2:3] + b_ref[...]


def _short_conv(u, w, b, batch, seq, tc=256):
    n, c = u.shape
    return pl.pallas_call(
        _short_conv_kernel, grid=(batch, c // tc),
        in_specs=[pl.BlockSpec((seq, tc), lambda bi, j: (bi, j)),
                  pl.BlockSpec((3, tc), lambda bi, j: (0, j)),
                  pl.BlockSpec((1, tc), lambda bi, j: (0, j))],
        out_specs=pl.BlockSpec((seq, tc), lambda bi, j: (bi, j)),
        out_shape=jax.ShapeDtypeStruct((n, c), F32),
        compiler_params=_params("parallel", "parallel"), name="hyena_short_conv")(u, w, b.reshape(1, c))


def _filter_hidden_kernel(z_ref, w1_ref, b1_ref, w2_ref, b2_ref, w3_ref, b3_ref, fr_ref, o_ref):
    hp = lax.Precision.HIGHEST
    fr = fr_ref[...]
    h = jnp.sin(fr * (jnp.dot(z_ref[...], w1_ref[...], precision=hp, preferred_element_type=F32) + b1_ref[...]))
    h = jnp.sin(fr * (jnp.dot(h, w2_ref[...], precision=hp, preferred_element_type=F32) + b2_ref[...]))
    o_ref[...] = jnp.sin(fr * (jnp.dot(h, w3_ref[...], precision=hp, preferred_element_type=F32) + b3_ref[...]))


def _filter_hidden(z_pad, w1, b1, w2, b2, w3, b3, freq):
    seq = z_pad.shape[0]

    def padm(w):
        return jnp.zeros((LANES, LANES), F32).at[:w.shape[0], :w.shape[1]].set(w)

    def padv(v):
        return jnp.zeros((1, LANES), F32).at[0, :v.shape[0]].set(v)

    return pl.pallas_call(
        _filter_hidden_kernel,
        out_shape=jax.ShapeDtypeStruct((seq, LANES), F32),
        compiler_params=pltpu.CompilerParams(vmem_limit_bytes=VMEM_LIMIT_BYTES),
        name="hyena_filter_hidden")(z_pad, padm(w1), padv(b1), padm(w2), padv(b2), padm(w3), padv(b3), padv(freq))


def _filter_pm_kernel(hid_ref, wf_ref, wb_ref, dec_ref, p_ref, m_ref):
    hp = lax.Precision.HIGHEST
    hid = hid_ref[...]
    dec = dec_ref[...]
    fwd = jnp.dot(hid, wf_ref[...], precision=hp, preferred_element_type=F32) * dec
    bwd = jnp.dot(hid, wb_ref[...], precision=hp, preferred_element_type=F32) * dec
    first = lax.broadcasted_iota(jnp.int32, fwd.shape, 0) == 0
    a = jnp.where(first, fwd + bwd, fwd)
    bt = jnp.where(first, 0.0, bwd)
    norm = jnp.sum(jnp.abs(a), 0, keepdims=True) + jnp.sum(jnp.abs(bt), 0, keepdims=True)
    inv = 1.0 / norm
    p_ref[...] = ((a + bt) * inv).astype(p_ref.dtype)
    m_ref[...] = ((a - bt) * inv).astype(m_ref.dtype)


def _filter_pm(hid, wout_pad, decay, tc=256):
    seq = hid.shape[0]
    cb = HY_WIDTH // tc
    out = jax.ShapeDtypeStruct((seq, HY_ORDER * HY_WIDTH), BF16)
    oblk = pl.BlockSpec((seq, tc), lambda n, j: (0, n * cb + j))
    return pl.pallas_call(
        _filter_pm_kernel, grid=(HY_ORDER, cb),
        in_specs=[pl.BlockSpec((seq, LANES), lambda n, j: (0, 0)),
                  pl.BlockSpec((LANES, tc), lambda n, j: (0, 2 * n * cb + j)),
                  pl.BlockSpec((LANES, tc), lambda n, j: (0, (2 * n + 1) * cb + j)),
                  pl.BlockSpec((seq, tc), lambda n, j: (0, j))],
        out_specs=[oblk, oblk], out_shape=[out, out],
        compiler_params=_params("parallel", "parallel"), name="hyena_filter_pm")(hid, wout_pad, wout_pad, decay)


def _dft_fwd_mul_kernel(fc_ref, fs_ref, z_ref, p_ref, q_ref, w1_ref, w2_ref):
    z = z_ref[...].astype(BF16)
    cu = _dot(fc_ref[...], z)
    su = _dot(fs_ref[...], z)
    p, q = p_ref[...], q_ref[...]
    w1_ref[0] = (cu * p - su * q).astype(w1_ref.dtype)
    w2_ref[0] = (cu * q + su * p).astype(w2_ref.dtype)


def _dft_fwd_mul(fc, fs, z, z_col_blk, pspec, qspec, order, batch, seq, tf=512, tc=256):
    cb = HY_WIDTH // tc
    out = jax.ShapeDtypeStruct((batch, seq, HY_WIDTH), BF16)
    oblk = pl.BlockSpec((1, tf, tc), lambda fi, b, j: (b, fi, j))
    kblk = pl.BlockSpec((tf, tc), lambda fi, b, j: (fi, order * cb + j))
    fblk = pl.BlockSpec((tf, seq), lambda fi, b, j: (fi, 0))
    return pl.pallas_call(
        _dft_fwd_mul_kernel, grid=(seq // tf, batch, cb),
        in_specs=[fblk, fblk,
                  pl.BlockSpec((seq, tc), lambda fi, b, j: (b, z_col_blk * cb + j)),
                  kblk, kblk],
        out_specs=[oblk, oblk], out_shape=[out, out],
        compiler_params=_params("parallel", "parallel", "parallel"),
        name="hyena_dft_fwd")(fc, fs, z, pspec, qspec)


def _dft_inv_gate_kernel(fct_ref, fst_ref, w1_ref, w2_ref, z_ref, gate_ref, skip_ref, o_ref, *, inv_len):
    y = (_dot(fct_ref[...], w1_ref[0]) + _dot(fst_ref[...], w2_ref[0])) * inv_len
    o_ref[...] = (gate_ref[...] * (y + z_ref[...] * skip_ref[...])).astype(o_ref.dtype)


def _dft_inv_gate(fct, fst, w1, w2, z, z_col_blk, gate, gate_col_blk, skip, out_dtype,
                  batch, seq, tt=512, tc=256):
    cb = HY_WIDTH // tc
    nt = seq // tt
    fblk = pl.BlockSpec((tt, seq), lambda ti, b, j: (ti, 0))
    wblk = pl.BlockSpec((1, seq, tc), lambda ti, b, j: (b, 0, j))
    return pl.pallas_call(
        functools.partial(_dft_inv_gate_kernel, inv_len=1.0 / seq), grid=(nt, batch, cb),
        in_specs=[fblk, fblk, wblk, wblk,
                  pl.BlockSpec((tt, tc), lambda ti, b, j: (b * nt + ti, z_col_blk * cb + j)),
                  pl.BlockSpec((tt, tc), lambda ti, b, j: (b * nt + ti, gate_col_blk * cb + j)),
                  pl.BlockSpec((1, tc), lambda ti, b, j: (0, j))],
        out_specs=pl.BlockSpec((tt, tc), lambda ti, b, j: (b * nt + ti, j)),
        out_shape=jax.ShapeDtypeStruct((batch * seq, HY_WIDTH), out_dtype),
        compiler_params=_params("parallel", "parallel", "parallel"),
        name="hyena_dft_inv")(fct, fst, w1, w2, z, gate, skip.reshape(1, HY_WIDTH))


def _merge_kernel(h_ref, wga_ref, wgb_ref, wgc_ref, gb_ref, oa_ref, ob_ref, oc_ref,
                  wa_ref, wb_ref, wc_ref, o_ref):
    hb = h_ref[...]
    acc = None
    for i, (wg, o, w) in enumerate(((wga_ref, oa_ref, wa_ref), (wgb_ref, ob_ref, wb_ref),
                                    (wgc_ref, oc_ref, wc_ref))):
        g = _dot(hb, wg[...]) + gb_ref[i]
        term = (1.0 / (1.0 + jnp.exp(-g))) * _dot(o[...], w[...])
        acc = term if acc is None else acc + term
    o_ref[...] = acc.astype(o_ref.dtype)


def _merge(hb, wgs, gate_b, outs, wbs, tm=512, tn=512):
    n, d = hb.shape
    row = lambda w: pl.BlockSpec((tm, w), lambda i, j: (i, 0))
    col = lambda k: pl.BlockSpec((k, tn), lambda i, j: (0, j))
    return pl.pallas_call(
        _merge_kernel, grid=(n // tm, d // tn),
        in_specs=[row(d), col(d), col(d), col(d),
                  pl.BlockSpec((N_BRANCH, 1, tn), lambda i, j: (0, 0, j)),
                  row(DA_WIDTH), row(HY_WIDTH), row(MLA_WIDTH),
                  col(DA_WIDTH), col(HY_WIDTH), col(MLA_WIDTH)],
        out_specs=pl.BlockSpec((tm, tn), lambda i, j: (i, j)),
        out_shape=jax.ShapeDtypeStruct((n, d), BF16),
        compiler_params=_params("parallel", "parallel"),
        name="gated_merge")(hb, *wgs, gate_b.reshape(N_BRANCH, 1, d), *outs, *wbs)


def _mm_res_ln_kernel(a_ref, b_ref, h_ref, g_ref, beta_ref, o_ref, ob_ref, acc_ref, *, nk):
    ki = pl.program_id(1)

    @pl.when(ki == 0)
    def _():
        acc_ref[...] = jnp.zeros(acc_ref.shape, F32)

    acc_ref[...] += _dot(a_ref[...], b_ref[...])

    @pl.when(ki == nk - 1)
    def _():
        y = _layer_norm(DEEPNORM_ALPHA * h_ref[...] + acc_ref[...], g_ref[...], beta_ref[...])
        o_ref[...] = y
        ob_ref[...] = y.astype(BF16)


def _matmul_res_ln(a, b, h, g, beta, tm=512, tk=1024, name="matmul_res_ln"):
    n, k = a.shape
    d = b.shape[1]
    nk = k // tk
    row = pl.BlockSpec((tm, d), lambda i, kk: (i, 0))
    vec = pl.BlockSpec((1, d), lambda i, kk: (0, 0))
    return pl.pallas_call(
        functools.partial(_mm_res_ln_kernel, nk=nk), grid=(n // tm, nk),
        in_specs=[pl.BlockSpec((tm, tk), lambda i, kk: (i, kk)),
                  pl.BlockSpec((tk, d), lambda i, kk: (kk, 0)), row, vec, vec],
        out_specs=[row, row],
        out_shape=[jax.ShapeDtypeStruct((n, d), F32), jax.ShapeDtypeStruct((n, d), BF16)],
        scratch_shapes=[pltpu.VMEM((tm, d), F32)],
        compiler_params=_params("parallel", "arbitrary"), name=name)(a, b, h, g.reshape(1, d), beta.reshape(1, d))


def _rope_tables(positions):
    inv = ROPE_THETA ** (-jnp.arange(0, ROPE_DIM, 2, dtype=F32) / ROPE_DIM)
    ang = positions.astype(F32).reshape(-1, 1) * inv
    cos, sin = jnp.cos(ang), jnp.sin(ang)
    zero = jnp.zeros_like(sin)
    c = jnp.concatenate([cos, cos, cos, cos], -1)
    sa = jnp.concatenate([-sin, zero, -sin, zero], -1)
    sb = jnp.concatenate([zero, sin, zero, sin], -1)
    return (c, sa, sb), (cos.T, sin.T)


def _hyena_features(seq_len):
    t = jnp.linspace(0.0, 1.0, seq_len, dtype=F32)[:, None]
    bands = (HY_EMB - 1) // 2
    w = 2.0 * math.pi * jnp.arange(seq_len, dtype=F32)[:, None] / seq_len
    f = jnp.linspace(1e-4, bands - 1, bands, dtype=F32)[None, :]
    ang = f * w
    z = jnp.concatenate([t, jnp.cos(ang), -jnp.sin(ang)], axis=-1)
    deltas = jnp.linspace(math.log(HY_DECAY_TARGET) / HY_FAST_DECAY,
                          math.log(HY_DECAY_TARGET) / HY_SLOW_DECAY, HY_WIDTH, dtype=F32)
    decay = jnp.exp(-t * jnp.abs(deltas)[None, :])
    z_pad = jnp.zeros((seq_len, LANES), F32).at[:, :HY_EMB].set(z)
    return z_pad, decay


def _dft_tables(seq_len):
    f = jnp.arange(seq_len, dtype=jnp.int32)[:, None]
    s = jnp.arange(seq_len, dtype=jnp.int32)[None, :]
    idx = ((2 * f + 1) * s) % (4 * seq_len)
    ang = idx.astype(F32) * (2.0 * math.pi / (4 * seq_len))
    fc = jnp.cos(ang).astype(BF16)
    fs = jnp.sin(ang).astype(BF16)
    return fc, fs, fc.T, fs.T


def kernel(x, positions, ln_emb_g, ln_emb_b, w_in, gate_b, da_lambda, da_subln_g, hy_conv_w, hy_conv_b, hy_f_w1, hy_f_b1, hy_f_w2, hy_f_b2, hy_f_w3, hy_f_b3, hy_f_freq, hy_f_wout, hy_skip, mla_q_norm_g, mla_kv_norm_g, mla_w_uq, mla_w_ukv, w_branch_a, w_branch_b, w_branch_c, w_out, ln1_g, ln1_b, mlp_w1, mlp_w2, ln2_g, ln2_b):
    batch, seq, d = x.shape
    n = batch * seq
    tabs, tabs_t = _rope_tables(positions)
    z_pad, decay = _hyena_features(seq)
    fc, fs, fct, fst = _dft_tables(seq)

    h, hb = _entry_ln(x.reshape(n, d), ln_emb_g, ln_emb_b)

    o_qk = 0
    o_v = 2 * DA_WIDTH
    o_hy = 3 * DA_WIDTH
    o_lat = o_hy + 3 * HY_WIDTH
    lat_w = MLA_Q_RANK + MLA_KV_RANK + MLA_ROPE
    o_g = o_lat + lat_w

    for l in range(DEPTH):
        lambda_init = 0.8 - 0.6 * math.exp(-0.3 * l)
        wl = w_in[l]
        da_scale = DA_HEAD_DIM ** -0.5 * LOG2_E
        wt_qv = jnp.concatenate([wl[:, :DA_WIDTH] * da_scale, wl[:, o_v:o_v + DA_WIDTH]], 1).T.astype(BF16)
        w_k = wl[:, DA_WIDTH:2 * DA_WIDTH].astype(BF16)
        w_hy = wl[:, o_hy:o_hy + 3 * HY_WIDTH].astype(BF16)
        w_lat = jnp.pad(wl[:, o_lat:o_lat + lat_w], ((0, 0), (0, LANES - MLA_ROPE))).astype(BF16)
        wgs = [wl[:, o_g + i * d:o_g + (i + 1) * d].astype(BF16) for i in range(N_BRANCH)]

        qvt = _proj_t(wt_qv, hb, tabs_t, DA_WIDTH)
        k_da = _matmul_rope(hb, w_k, tabs)
        lq1, lk1, lq2, lk2 = da_lambda[l].astype(F32)
        lam = jnp.exp(jnp.sum(lq1 * lk1)) - jnp.exp(jnp.sum(lq2 * lk2)) + lambda_init
        lam2 = jnp.stack([lam, jnp.asarray(1.0 - lambda_init, F32)]).astype(F32)
        o_a = _attention_t(lam2, qvt, k_da, qvt, DA_HEADS, da_subln_g[l].reshape(-1, 1), 2, batch, seq,
                           DA_HEADS, 2 * DA_HEAD_DIM, 2 * DA_HEAD_DIM, 512, 512, "diff_attention")

        u_hy = _matmul(hb, w_hy, F32, 512, 1024, name="hyena_in_proj")
        uc = _short_conv(u_hy, hy_conv_w[l], hy_conv_b[l], batch, seq)
        hid = _filter_hidden(z_pad, hy_f_w1[l], hy_f_b1[l], hy_f_w2[l], hy_f_b2[l],
                             hy_f_w3[l], hy_f_b3[l], hy_f_freq[l])
        wout_pad = jnp.zeros((LANES, 2 * HY_ORDER * HY_WIDTH), F32).at[:HY_FILTER_WIDTH].set(hy_f_wout[l])
        p_filt, m_filt = _filter_pm(hid, wout_pad, decay)
        pspec = _matmul(fc, p_filt, F32, 512, 1024, name="hyena_filter_dft_c")
        qspec = _matmul(fs, m_filt, F32, 512, 1024, name="hyena_filter_dft_s")
        w1, w2 = _dft_fwd_mul(fc, fs, uc, 0, pspec, qspec, 0, batch, seq)
        z1 = _dft_inv_gate(fct, fst, w1, w2, uc, 0, uc, 1, hy_skip[l, 0], F32, batch, seq)
        w1, w2 = _dft_fwd_mul(fc, fs, z1, 0, pspec, qspec, 1, batch, seq)
        o_b = _dft_inv_gate(fct, fst, w1, w2, z1, 0, uc, 2, hy_skip[l, 1], BF16, batch, seq)

        lat = _matmul(hb, w_lat, F32, 512, w_lat.shape[1], name="mla_latent_proj")
        wq = mla_w_uq[l].reshape(MLA_Q_RANK, MLA_HEADS, MLA_NOPE + MLA_ROPE)
        wq = jnp.pad(wq, ((0, 0), (0, 0), (0, MLA_QK_PAD - MLA_NOPE - MLA_ROPE)))
        wqt = wq.reshape(MLA_Q_RANK, MLA_HEADS * MLA_QK_PAD).T.astype(BF16)
        wkv = mla_w_ukv[l].reshape(MLA_KV_RANK, MLA_HEADS, MLA_NOPE + MLA_V)
        wk = wkv[:, :, :MLA_NOPE].reshape(MLA_KV_RANK, MLA_WIDTH).astype(BF16)
        wvt = wkv[:, :, MLA_NOPE:].reshape(MLA_KV_RANK, MLA_WIDTH).T.astype(BF16)
        mla_scale = (MLA_NOPE + MLA_ROPE) ** -0.5 * LOG2_E
        qt_c = _mla_q_t(lat, mla_q_norm_g[l], wqt, tabs_t, mla_scale)
        k_cat, vt_c = _mla_kv(lat, mla_kv_norm_g[l], wk, wvt, tabs)
        o_c = _attention_t(lam2, qt_c, k_cat, vt_c, 0, jnp.ones((MLA_V, 1), F32), 1, batch, seq,
                           MLA_HEADS, MLA_QK_PAD, MLA_V, 1024, 512, "mla_attention")

        merged = _merge(hb, wgs, gate_b[l], (o_a, o_b, o_c),
                        (w_branch_a[l].astype(BF16), w_branch_b[l].astype(BF16), w_branch_c[l].astype(BF16)))
        h, hb = _matmul_res_ln(merged, w_out[l].astype(BF16), h, ln1_g[l], ln1_b[l], tk=d, name="out_proj_ln")

        ff = _matmul(hb, mlp_w1[l].astype(BF16), BF16, 1024, 1024, square_relu=True, name="mlp_up")
        h, hb = _matmul_res_ln(ff, mlp_w2[l].astype(BF16), h, ln2_g[l], ln2_b[l], name="mlp_down_ln")

    return h.reshape(batch, seq, d)
```

```python
import functools
import math

import jax
import jax.numpy as jnp
from jax import lax
from jax.experimental import pallas as pl
from jax.experimental.pallas import tpu as pltpu

D_MODEL = 2048
DEPTH = 4
ROPE_DIM = 64
ROPE_THETA = 10000.0
DA_HEADS = 8
DA_HEAD_DIM = 64
DA_WIDTH = DA_HEADS * 2 * DA_HEAD_DIM
HY_WIDTH = D_MODEL // 2
HY_ORDER = 2
HY_EMB = 33
HY_FILTER_WIDTH = 64
HY_DECAY_TARGET = 1e-2
HY_FAST_DECAY = 0.3
HY_SLOW_DECAY = 1.5
MLA_HEADS = 8
MLA_NOPE = 128
MLA_ROPE = 64
MLA_V = 128
MLA_Q_RANK = D_MODEL // 4
MLA_KV_RANK = D_MODEL // 8
MLA_WIDTH = MLA_HEADS * MLA_V
N_BRANCH = 3
D_FF = 4 * D_MODEL
DEEPNORM_ALPHA = (2 * DEPTH) ** 0.25
LN_EPS = 1e-5
RMS_EPS = 1e-6

LANES = 128
BF16_SUBLANES = 16
MLA_QK_PAD = 2 * LANES
LOG2_E = math.log2(math.e)
HY_SEGMENT = 1024
VMEM_LIMIT_BYTES = 56 * 1024 * 1024

BF16 = jnp.bfloat16
F32 = jnp.float32


def _params(*semantics):
    return pltpu.CompilerParams(dimension_semantics=semantics,
                                vmem_limit_bytes=VMEM_LIMIT_BYTES)


def _dot(a, b):
    return jnp.dot(a, b, preferred_element_type=F32)


def _dot_nt(a, b):
    return lax.dot_general(a, b, (((1,), (1,)), ((), ())), preferred_element_type=F32)


def _layer_norm(v, g, b):
    mu = jnp.mean(v, -1, keepdims=True)
    d = v - mu
    var = jnp.mean(d * d, -1, keepdims=True)
    return d * lax.rsqrt(var + LN_EPS) * g + b


def _rope_lanes(x, c, sa, sb):
    return x * c + pltpu.roll(x, LANES - 32, 1) * sa + pltpu.roll(x, 32, 1) * sb


def _ln_kernel(x_ref, g_ref, b_ref, o_ref, ob_ref):
    y = _layer_norm(x_ref[...], g_ref[...], b_ref[...])
    o_ref[...] = y
    ob_ref[...] = y.astype(BF16)


def _entry_ln(x, g, b, tm=512):
    n, d = x.shape
    row = pl.BlockSpec((tm, d), lambda i: (i, 0))
    vec = pl.BlockSpec((1, d), lambda i: (0, 0))
    return pl.pallas_call(
        _ln_kernel, grid=(n // tm,), in_specs=[row, vec, vec], out_specs=[row, row],
        out_shape=[jax.ShapeDtypeStruct((n, d), F32), jax.ShapeDtypeStruct((n, d), BF16)],
        compiler_params=_params("parallel"), name="entry_ln")(x, g.reshape(1, d), b.reshape(1, d))


def _mm_kernel(a_ref, b_ref, o_ref, *, square_relu):
    acc = _dot(a_ref[...], b_ref[...])
    if square_relu:
        acc = jnp.square(jnp.maximum(acc, 0.0))
    o_ref[...] = acc.astype(o_ref.dtype)


def _matmul(a, b, out_dtype, tm, tn, square_relu=False, name="matmul"):
    m, k = a.shape
    _, n = b.shape
    return pl.pallas_call(
        functools.partial(_mm_kernel, square_relu=square_relu),
        grid=(m // tm, n // tn),
        in_specs=[pl.BlockSpec((tm, k), lambda i, j: (i, 0)),
                  pl.BlockSpec((k, tn), lambda i, j: (0, j))],
        out_specs=pl.BlockSpec((tm, tn), lambda i, j: (i, j)),
        out_shape=jax.ShapeDtypeStruct((m, n), out_dtype),
        compiler_params=_params("parallel", "parallel"), name=name)(a, b)


def _mm_rope_kernel(a_ref, b_ref, c_ref, sa_ref, sb_ref, o_ref):
    acc = _dot(a_ref[...], b_ref[...])
    c, sa, sb = c_ref[...], sa_ref[...], sb_ref[...]
    for j in range(acc.shape[1] // LANES):
        sl = slice(j * LANES, (j + 1) * LANES)
        o_ref[:, sl] = _rope_lanes(acc[:, sl], c, sa, sb).astype(o_ref.dtype)


def _matmul_rope(a, b, tabs, tm=512):
    m, k = a.shape
    _, n = b.shape
    tab = pl.BlockSpec((tm, LANES), lambda i: (i, 0))
    return pl.pallas_call(
        _mm_rope_kernel, grid=(m // tm,),
        in_specs=[pl.BlockSpec((tm, k), lambda i: (i, 0)),
                  pl.BlockSpec((k, n), lambda i: (0, 0)), tab, tab, tab],
        out_specs=pl.BlockSpec((tm, n), lambda i: (i, 0)),
        out_shape=jax.ShapeDtypeStruct((m, n), BF16),
        compiler_params=_params("parallel"), name="qk_rope_proj")(a, b, *tabs)


def _rope_rows(x1, x2, cos_t, sin_t):
    return x1 * cos_t - x2 * sin_t, x2 * cos_t + x1 * sin_t


def _proj_t_kernel(wt_ref, h_ref, cos_ref, sin_ref, o_ref, *, rope_rows):
    acc = _dot_nt(wt_ref[...], h_ref[...])
    cos_t, sin_t = cos_ref[...], sin_ref[...]
    half = ROPE_DIM // 2
    for g in range(rope_rows // ROPE_DIM):
        r = g * ROPE_DIM
        y1, y2 = _rope_rows(acc[r:r + half], acc[r + half:r + ROPE_DIM], cos_t, sin_t)
        o_ref[r:r + half, :] = y1.astype(o_ref.dtype)
        o_ref[r + half:r + ROPE_DIM, :] = y2.astype(o_ref.dtype)
    o_ref[rope_rows:, :] = acc[rope_rows:].astype(o_ref.dtype)


def _proj_t(wt, hb, tabs_t, rope_rows, tm=512):
    f, k = wt.shape
    n = hb.shape[0]
    tab = pl.BlockSpec((ROPE_DIM // 2, tm), lambda i: (0, i))
    return pl.pallas_call(
        functools.partial(_proj_t_kernel, rope_rows=rope_rows), grid=(n // tm,),
        in_specs=[pl.BlockSpec((f, k), lambda i: (0, 0)),
                  pl.BlockSpec((tm, k), lambda i: (i, 0)), tab, tab],
        out_specs=pl.BlockSpec((f, tm), lambda i: (0, i)),
        out_shape=jax.ShapeDtypeStruct((f, n), BF16),
        compiler_params=_params("parallel"), name="da_qv_proj_t")(wt, hb, *tabs_t)


def _attn_t_kernel(lam_ref, qt_ref, k_ref, vt_ref, g_ref, o_ref, *, n_maps, tk, nk):
    qt = qt_ref[...]
    tq = qt.shape[1]
    if n_maps == 2:
        row = lax.broadcasted_iota(jnp.int32, qt.shape, 0)
        zero = jnp.zeros_like(qt)
        split = qt.shape[0] // 2
        qcat = jnp.concatenate([jnp.where(row < split, qt, zero), jnp.where(row >= split, qt, zero)], axis=1)
    else:
        qcat = qt
    w = qcat.shape[1]
    dv = vt_ref.shape[0]
    ones = jnp.ones((BF16_SUBLANES, tk), BF16)
    m = jnp.full((1, w), -jnp.inf, F32)
    acc = jnp.zeros((dv + BF16_SUBLANES, w), F32)
    for j in range(nk):
        kj = k_ref[j * tk:(j + 1) * tk, :]
        vj = jnp.concatenate([vt_ref[:, j * tk:(j + 1) * tk], ones], axis=0)
        s = _dot(kj, qcat)
        m_new = jnp.maximum(m, jnp.max(s, axis=0, keepdims=True))
        alpha = jnp.exp2(m - m_new)
        p = jnp.exp2(s - m_new).astype(BF16)
        acc = acc * alpha + _dot(vj, p)
        m = m_new
    o = acc[:dv] / acc[dv:dv + 1]
    if n_maps == 2:
        o = o[:, :tq] - lam_ref[0] * o[:, tq:]
        ms = jnp.mean(o * o, axis=0, keepdims=True)
        o = o * lax.rsqrt(ms + LN_EPS) * g_ref[...] * lam_ref[1]
    o_ref[...] = o.T.astype(o_ref.dtype)


def _attention_t(lam2, qt, k, vt, v_row_blk, g, n_maps, batch, seq, heads, dqk, dv, tq, tk, name):
    nq, nk = seq // tq, seq // tk
    return pl.pallas_call(
        functools.partial(_attn_t_kernel, n_maps=n_maps, tk=tk, nk=nk),
        grid=(batch, heads, nq),
        in_specs=[pl.BlockSpec(memory_space=pltpu.SMEM),
                  pl.BlockSpec((dqk, tq), lambda b, h, qi: (h, b * nq + qi)),
                  pl.BlockSpec((seq, dqk), lambda b, h, qi: (b, h)),
                  pl.BlockSpec((dv, seq), lambda b, h, qi: (v_row_blk + h, b)),
                  pl.BlockSpec((dv, 1), lambda b, h, qi: (0, 0))],
        out_specs=pl.BlockSpec((tq, dv), lambda b, h, qi: (b * nq + qi, h)),
        out_shape=jax.ShapeDtypeStruct((batch * seq, heads * dv), BF16),
        compiler_params=_params("parallel", "parallel", "parallel"),
        name=name)(lam2, qt, k, vt, g)


def _rms_norm(x, g):
    return x * lax.rsqrt(jnp.mean(x * x, -1, keepdims=True) + RMS_EPS) * g


def _mla_q_t_kernel(cq_ref, g_ref, wt_ref, cos_ref, sin_ref, o_ref, *, scale):
    xn = _rms_norm(cq_ref[...], g_ref[...]).astype(BF16)
    acc = _dot_nt(wt_ref[...], xn) * scale
    cos_t, sin_t = cos_ref[...], sin_ref[...]
    half = MLA_ROPE // 2
    for h in range(MLA_HEADS):
        base = h * MLA_QK_PAD
        r = base + MLA_NOPE
        o_ref[base:r, :] = acc[base:r].astype(o_ref.dtype)
        y1, y2 = _rope_rows(acc[r:r + half], acc[r + half:r + MLA_ROPE], cos_t, sin_t)
        o_ref[r:r + half, :] = y1.astype(o_ref.dtype)
        o_ref[r + half:r + MLA_ROPE, :] = y2.astype(o_ref.dtype)
        o_ref[r + MLA_ROPE:base + MLA_QK_PAD, :] = acc[r + MLA_ROPE:base + MLA_QK_PAD].astype(o_ref.dtype)


def _mla_q_t(lat, g, wt, tabs_t, scale, tm=512):
    n = lat.shape[0]
    tab = pl.BlockSpec((MLA_ROPE // 2, tm), lambda i: (0, i))
    wd = MLA_HEADS * MLA_QK_PAD
    return pl.pallas_call(
        functools.partial(_mla_q_t_kernel, scale=scale), grid=(n // tm,),
        in_specs=[pl.BlockSpec((tm, MLA_Q_RANK), lambda i: (i, 0)),
                  pl.BlockSpec((1, MLA_Q_RANK), lambda i: (0, 0)),
                  pl.BlockSpec((wd, MLA_Q_RANK), lambda i: (0, 0)), tab, tab],
        out_specs=pl.BlockSpec((wd, tm), lambda i: (0, i)),
        out_shape=jax.ShapeDtypeStruct((wd, n), BF16),
        compiler_params=_params("parallel"), name="mla_q_up_t")(lat, g.reshape(1, -1), wt, *tabs_t)


def _mla_kv_kernel(ckv_ref, kr_ref, g_ref, wk_ref, wvt_ref, c_ref, sa_ref, sb_ref, k_ref, vt_ref):
    xn = _rms_norm(ckv_ref[...], g_ref[...]).astype(BF16)
    kn = _dot(xn, wk_ref[...])
    vt_ref[...] = _dot_nt(wvt_ref[...], xn).astype(vt_ref.dtype)
    kr = _rope_lanes(kr_ref[...], c_ref[...], sa_ref[...], sb_ref[...]).astype(k_ref.dtype)
    for h in range(MLA_HEADS):
        base = h * MLA_QK_PAD
        k_ref[:, base:base + LANES] = kn[:, h * MLA_NOPE:(h + 1) * MLA_NOPE].astype(k_ref.dtype)
        k_ref[:, base + LANES:base + 2 * LANES] = kr


def _mla_kv(lat, g, wk, wvt, tabs, tm=512):
    n = lat.shape[0]
    tab = pl.BlockSpec((tm, LANES), lambda i: (i, 0))
    wd = MLA_HEADS * MLA_QK_PAD
    ckv_blk = MLA_Q_RANK // MLA_KV_RANK
    kr_blk = (MLA_Q_RANK + MLA_KV_RANK) // LANES
    return pl.pallas_call(
        _mla_kv_kernel, grid=(n // tm,),
        in_specs=[pl.BlockSpec((tm, MLA_KV_RANK), lambda i: (i, ckv_blk)),
                  pl.BlockSpec((tm, LANES), lambda i: (i, kr_blk)),
                  pl.BlockSpec((1, MLA_KV_RANK), lambda i: (0, 0)),
                  pl.BlockSpec((MLA_KV_RANK, MLA_WIDTH), lambda i: (0, 0)),
                  pl.BlockSpec((MLA_WIDTH, MLA_KV_RANK), lambda i: (0, 0)), tab, tab, tab],
        out_specs=[pl.BlockSpec((tm, wd), lambda i: (i, 0)),
                   pl.BlockSpec((MLA_WIDTH, tm), lambda i: (0, i))],
        out_shape=[jax.ShapeDtypeStruct((n, wd), BF16), jax.ShapeDtypeStruct((MLA_WIDTH, n), BF16)],
        compiler_params=_params("parallel"), name="mla_kv_up")(lat, lat, g.reshape(1, -1), wk, wvt, *tabs)


def _short_conv_kernel(u_ref, w_ref, b_ref, o_ref):
    u = u_ref[...]
    seq = u.shape[0]
    row = lax.broadcasted_iota(jnp.int32, u.shape, 0)
    prev = jnp.where(row == 0, 0.0, pltpu.roll(u, 1, 0))
    nxt = jnp.where(row == seq - 1, 0.0, pltpu.roll(u, seq - 1, 0))
    w = w_ref[...]
    o_ref[...] = prev * w[0:1] + u * w[1:2] + nxt * w[2:3] + b_ref[...]


def _short_conv(u, w, b, batch, seq, tc=256):
    n, c = u.shape
    return pl.pallas_call(
        _short_conv_kernel, grid=(batch, c // tc),
        in_specs=[pl.BlockSpec((seq, tc), lambda bi, j: (bi, j)),
                  pl.BlockSpec((3, tc), lambda bi, j: (0, j)),
                  pl.BlockSpec((1, tc), lambda bi, j: (0, j))],
        out_specs=pl.BlockSpec((seq, tc), lambda bi, j: (bi, j)),
        out_shape=jax.ShapeDtypeStruct((n, c), F32),
        compiler_params=_params("parallel", "parallel"), name="hyena_short_conv")(u, w, b.reshape(1, c))


def _filter_hidden_kernel(z_ref, w1_ref, b1_ref, w2_ref, b2_ref, w3_ref, b3_ref, fr_ref, o_ref):
    hp = lax.Precision.HIGHEST
    fr = fr_ref[...]
    h = jnp.sin(fr * (jnp.dot(z_ref[...], w1_ref[...], precision=hp, preferred_element_type=F32) + b1_ref[...]))
    h = jnp.sin(fr * (jnp.dot(h, w2_ref[...], precision=hp, preferred_element_type=F32) + b2_ref[...]))
    o_ref[...] = jnp.sin(fr * (jnp.dot(h, w3_ref[...], precision=hp, preferred_element_type=F32) + b3_ref[...]))


def _filter_hidden(z_pad, w1, b1, w2, b2, w3, b3, freq):
    seq = z_pad.shape[0]

    def padm(w):
        return jnp.zeros((LANES, LANES), F32).at[:w.shape[0], :w.shape[1]].set(w)

    def padv(v):
        return jnp.zeros((1, LANES), F32).at[0, :v.shape[0]].set(v)

    return pl.pallas_call(
        _filter_hidden_kernel,
        out_shape=jax.ShapeDtypeStruct((seq, LANES), F32),
        compiler_params=pltpu.CompilerParams(vmem_limit_bytes=VMEM_LIMIT_BYTES),
        name="hyena_filter_hidden")(z_pad, padm(w1), padv(b1), padm(w2), padv(b2), padm(w3), padv(b3), padv(freq))


def _filter_pm_kernel(hid_ref, wf_ref, wb_ref, dec_ref, a_ref, b_ref):
    hp = lax.Precision.HIGHEST
    hid = hid_ref[...]
    dec = dec_ref[...]
    fwd = jnp.dot(hid, wf_ref[...], precision=hp, preferred_element_type=F32) * dec
    bwd = jnp.dot(hid, wb_ref[...], precision=hp, preferred_element_type=F32) * dec
    first = lax.broadcasted_iota(jnp.int32, fwd.shape, 0) == 0
    a = jnp.where(first, fwd + bwd, fwd)
    bt = jnp.where(first, 0.0, bwd)
    norm = jnp.sum(jnp.abs(a), 0, keepdims=True) + jnp.sum(jnp.abs(bt), 0, keepdims=True)
    inv = 1.0 / norm
    seq = a.shape[0]
    lead = a_ref.shape[0] - seq
    zeros = jnp.zeros((lead, a.shape[1]), a_ref.dtype)
    a_ref[:lead, :] = zeros
    b_ref[:lead, :] = zeros
    a_ref[lead:, :] = (a * inv).astype(a_ref.dtype)
    b_ref[lead:, :] = (bt * inv).astype(b_ref.dtype)


def _filter_taps(hid, wout_pad, decay, seg, tc=256):
    seq = hid.shape[0]
    cb = HY_WIDTH // tc
    out = jax.ShapeDtypeStruct((seg + seq, HY_ORDER * HY_WIDTH), BF16)
    oblk = pl.BlockSpec((seg + seq, tc), lambda n, j: (0, n * cb + j))
    return pl.pallas_call(
        _filter_pm_kernel, grid=(HY_ORDER, cb),
        in_specs=[pl.BlockSpec((seq, LANES), lambda n, j: (0, 0)),
                  pl.BlockSpec((LANES, tc), lambda n, j: (0, 2 * n * cb + j)),
                  pl.BlockSpec((LANES, tc), lambda n, j: (0, (2 * n + 1) * cb + j)),
                  pl.BlockSpec((seq, tc), lambda n, j: (0, j))],
        out_specs=[oblk, oblk], out_shape=[out, out],
        compiler_params=_params("parallel", "parallel"), name="hyena_filter_taps")(hid, wout_pad, wout_pad, decay)


def _filter_spec_kernel(tcr_ref, tc_ref, tsr_ref, ts_ref, lo_ref, hi_ref, u_ref, v_ref):
    lo, hi = lo_ref[...], hi_ref[...]
    u_ref[0] = _dot(tcr_ref[...], lo) + _dot(tc_ref[...], hi)
    v_ref[0] = _dot(tsr_ref[...], lo) + _dot(ts_ref[...], hi)


def _filter_spec(tabs, taps, nb, seg, tf=512, tcol=512):
    fc, fs, fcr, fsr_neg = tabs
    cols = taps.shape[1]
    tblk = pl.BlockSpec((tf, seg), lambda q, fi, j: (fi, 0))
    out = jax.ShapeDtypeStruct((nb, seg, cols), F32)
    oblk = pl.BlockSpec((1, tf, tcol), lambda q, fi, j: (q, fi, j))
    return pl.pallas_call(
        _filter_spec_kernel, grid=(nb, seg // tf, cols // tcol),
        in_specs=[tblk, tblk, tblk, tblk,
                  pl.BlockSpec((seg, tcol), lambda q, fi, j: (q, j)),
                  pl.BlockSpec((seg, tcol), lambda q, fi, j: (q + 1, j))],
        out_specs=[oblk, oblk], out_shape=[out, out],
        compiler_params=_params("parallel", "parallel", "parallel"),
        name="hyena_filter_spec")(fcr, fc, fsr_neg, fs, taps, taps)


def _dft_fwd_mul_kernel(fc_ref, fs_ref, z_ref, ua_ref, va_ref, ub_ref, vb_ref, w1_ref, w2_ref, *, nb):
    seg = z_ref.shape[0] // nb
    fc, fs = fc_ref[...], fs_ref[...]
    cus, sus = [], []
    for i in range(nb):
        zi = z_ref[i * seg:(i + 1) * seg, :].astype(BF16)
        cus.append(_dot(fc, zi))
        sus.append(_dot(fs, zi))
    for j in range(nb):
        w1 = w2 = None
        for i in range(nb):
            o = j - i
            cu, su = cus[i], sus[i]
            if o > 0:
                p, q = ua_ref[o], va_ref[o]
                t1, t2 = cu * p - su * q, cu * q + su * p
            elif o < 0:
                p, q = ub_ref[-o], vb_ref[-o]
                t1, t2 = cu * p + su * q, su * p - cu * q
            else:
                p, q = ua_ref[0] + ub_ref[0], va_ref[0] - vb_ref[0]
                t1, t2 = cu * p - su * q, cu * q + su * p
            w1 = t1 if w1 is None else w1 + t1
            w2 = t2 if w2 is None else w2 + t2
        w1_ref[j] = w1.astype(w1_ref.dtype)
        w2_ref[j] = w2.astype(w2_ref.dtype)


def _dft_fwd_mul(fc, fs, z, z_col_blk, spec_a, spec_b, order, batch, seq, nb, tf=512, tc=256):
    cb = HY_WIDTH // tc
    seg = seq // nb
    ua, va = spec_a
    ub, vb = spec_b
    out = jax.ShapeDtypeStruct((batch * nb, seg, HY_WIDTH), BF16)
    oblk = pl.BlockSpec((nb, tf, tc), lambda fi, j, b: (b, fi, j))
    kblk = pl.BlockSpec((nb, tf, tc), lambda fi, j, b: (0, fi, order * cb + j))
    fblk = pl.BlockSpec((tf, seg), lambda fi, j, b: (fi, 0))
    return pl.pallas_call(
        functools.partial(_dft_fwd_mul_kernel, nb=nb), grid=(seg // tf, cb, batch),
        in_specs=[fblk, fblk,
                  pl.BlockSpec((seq, tc), lambda fi, j, b: (b, z_col_blk * cb + j)),
                  kblk, kblk, kblk, kblk],
        out_specs=[oblk, oblk], out_shape=[out, out],
        compiler_params=_params("parallel", "parallel", "parallel"),
        name="hyena_dft_fwd")(fc, fs, z, ua, va, ub, vb)


def _dft_inv_gate_kernel(fct_ref, fst_ref, w1_ref, w2_ref, z_ref, gate_ref, skip_ref, o_ref, *, inv_len):
    y = (_dot(fct_ref[...], w1_ref[0]) + _dot(fst_ref[...], w2_ref[0])) * inv_len
    o_ref[...] = (gate_ref[...] * (y + z_ref[...] * skip_ref[...])).astype(o_ref.dtype)


def _dft_inv_gate(fct, fst, w1, w2, z, z_col_blk, gate, gate_col_blk, skip, out_dtype,
                  batch, seq, tc=512):
    tt = min(seq, 1024)
    cb = HY_WIDTH // tc
    nt = seq // tt
    fblk = pl.BlockSpec((tt, seq), lambda ti, b, j: (ti, 0))
    wblk = pl.BlockSpec((1, seq, tc), lambda ti, b, j: (b, 0, j))
    return pl.pallas_call(
        functools.partial(_dft_inv_gate_kernel, inv_len=1.0 / seq), grid=(nt, batch, cb),
        in_specs=[fblk, fblk, wblk, wblk,
                  pl.BlockSpec((tt, tc), lambda ti, b, j: (b * nt + ti, z_col_blk * cb + j)),
                  pl.BlockSpec((tt, tc), lambda ti, b, j: (b * nt + ti, gate_col_blk * cb + j)),
                  pl.BlockSpec((1, tc), lambda ti, b, j: (0, j))],
        out_specs=pl.BlockSpec((tt, tc), lambda ti, b, j: (b * nt + ti, j)),
        out_shape=jax.ShapeDtypeStruct((batch * seq, HY_WIDTH), out_dtype),
        compiler_params=_params("parallel", "parallel", "parallel"),
        name="hyena_dft_inv")(fct, fst, w1, w2, z, gate, skip.reshape(1, HY_WIDTH))


def _merge_kernel(h_ref, wga_ref, wgb_ref, wgc_ref, gb_ref, oa_ref, ob_ref, oc_ref,
                  wa_ref, wb_ref, wc_ref, o_ref):
    hb = h_ref[...]
    acc = None
    for i, (wg, o, w) in enumerate(((wga_ref, oa_ref, wa_ref), (wgb_ref, ob_ref, wb_ref),
                                    (wgc_ref, oc_ref, wc_ref))):
        g = _dot(hb, wg[...]) + gb_ref[i]
        term = (1.0 / (1.0 + jnp.exp(-g))) * _dot(o[...], w[...])
        acc = term if acc is None else acc + term
    o_ref[...] = acc.astype(o_ref.dtype)


def _merge(hb, wgs, gate_b, outs, wbs, tm=512, tn=512):
    n, d = hb.shape
    row = lambda w: pl.BlockSpec((tm, w), lambda i, j: (i, 0))
    col = lambda k: pl.BlockSpec((k, tn), lambda i, j: (0, j))
    return pl.pallas_call(
        _merge_kernel, grid=(n // tm, d // tn),
        in_specs=[row(d), col(d), col(d), col(d),
                  pl.BlockSpec((N_BRANCH, 1, tn), lambda i, j: (0, 0, j)),
                  row(DA_WIDTH), row(HY_WIDTH), row(MLA_WIDTH),
                  col(DA_WIDTH), col(HY_WIDTH), col(MLA_WIDTH)],
        out_specs=pl.BlockSpec((tm, tn), lambda i, j: (i, j)),
        out_shape=jax.ShapeDtypeStruct((n, d), BF16),
        compiler_params=_params("parallel", "parallel"),
        name="gated_merge")(hb, *wgs, gate_b.reshape(N_BRANCH, 1, d), *outs, *wbs)


def _mm_res_ln_kernel(a_ref, b_ref, h_ref, g_ref, beta_ref, o_ref, ob_ref, acc_ref, *, nk):
    ki = pl.program_id(1)

    @pl.when(ki == 0)
    def _():
        acc_ref[...] = jnp.zeros(acc_ref.shape, F32)

    acc_ref[...] += _dot(a_ref[...], b_ref[...])

    @pl.when(ki == nk - 1)
    def _():
        y = _layer_norm(DEEPNORM_ALPHA * h_ref[...] + acc_ref[...], g_ref[...], beta_ref[...])
        o_ref[...] = y
        ob_ref[...] = y.astype(BF16)


def _matmul_res_ln(a, b, h, g, beta, tm=512, tk=1024, name="matmul_res_ln"):
    n, k = a.shape
    d = b.shape[1]
    nk = k // tk
    row = pl.BlockSpec((tm, d), lambda i, kk: (i, 0))
    vec = pl.BlockSpec((1, d), lambda i, kk: (0, 0))
    return pl.pallas_call(
        functools.partial(_mm_res_ln_kernel, nk=nk), grid=(n // tm, nk),
        in_specs=[pl.BlockSpec((tm, tk), lambda i, kk: (i, kk)),
                  pl.BlockSpec((tk, d), lambda i, kk: (kk, 0)), row, vec, vec],
        out_specs=[row, row],
        out_shape=[jax.ShapeDtypeStruct((n, d), F32), jax.ShapeDtypeStruct((n, d), BF16)],
        scratch_shapes=[pltpu.VMEM((tm, d), F32)],
        compiler_params=_params("parallel", "arbitrary"), name=name)(a, b, h, g.reshape(1, d), beta.reshape(1, d))


def _rope_tables(positions):
    inv = ROPE_THETA ** (-jnp.arange(0, ROPE_DIM, 2, dtype=F32) / ROPE_DIM)
    ang = positions.astype(F32).reshape(-1, 1) * inv
    cos, sin = jnp.cos(ang), jnp.sin(ang)
    zero = jnp.zeros_like(sin)
    c = jnp.concatenate([cos, cos, cos, cos], -1)
    sa = jnp.concatenate([-sin, zero, -sin, zero], -1)
    sb = jnp.concatenate([zero, sin, zero, sin], -1)
    return (c, sa, sb), (cos.T, sin.T)


def _hyena_features(seq_len):
    t = jnp.linspace(0.0, 1.0, seq_len, dtype=F32)[:, None]
    bands = (HY_EMB - 1) // 2
    w = 2.0 * math.pi * jnp.arange(seq_len, dtype=F32)[:, None] / seq_len
    f = jnp.linspace(1e-4, bands - 1, bands, dtype=F32)[None, :]
    ang = f * w
    z = jnp.concatenate([t, jnp.cos(ang), -jnp.sin(ang)], axis=-1)
    deltas = jnp.linspace(math.log(HY_DECAY_TARGET) / HY_FAST_DECAY,
                          math.log(HY_DECAY_TARGET) / HY_SLOW_DECAY, HY_WIDTH, dtype=F32)
    decay = jnp.exp(-t * jnp.abs(deltas)[None, :])
    z_pad = jnp.zeros((seq_len, LANES), F32).at[:, :HY_EMB].set(z)
    return z_pad, decay


def _dft_tables(seg):
    f = jnp.arange(seg, dtype=jnp.int32)[:, None]
    s = jnp.arange(seg, dtype=jnp.int32)[None, :]
    scale = 2.0 * math.pi / (4 * seg)
    ang = (((2 * f + 1) * s) % (4 * seg)).astype(F32) * scale
    ang_r = (((2 * f + 1) * (seg - s)) % (4 * seg)).astype(F32) * scale
    fc = jnp.cos(ang).astype(BF16)
    fs = jnp.sin(ang).astype(BF16)
    fcr = jnp.where(s == 0, 0.0, jnp.cos(ang_r)).astype(BF16)
    fsr_neg = jnp.where(s == 0, 0.0, -jnp.sin(ang_r)).astype(BF16)
    return (fc, fs, fcr, fsr_neg), (fc.T, fs.T)


def kernel(x, positions, ln_emb_g, ln_emb_b, w_in, gate_b, da_lambda, da_subln_g, hy_conv_w, hy_conv_b, hy_f_w1, hy_f_b1, hy_f_w2, hy_f_b2, hy_f_w3, hy_f_b3, hy_f_freq, hy_f_wout, hy_skip, mla_q_norm_g, mla_kv_norm_g, mla_w_uq, mla_w_ukv, w_branch_a, w_branch_b, w_branch_c, w_out, ln1_g, ln1_b, mlp_w1, mlp_w2, ln2_g, ln2_b):
    batch, seq, d = x.shape
    n = batch * seq
    tabs, tabs_t = _rope_tables(positions)
    z_pad, decay = _hyena_features(seq)
    seg = min(seq, HY_SEGMENT)
    nb = seq // seg
    dft_fwd, (fct, fst) = _dft_tables(seg)
    fc, fs = dft_fwd[:2]

    h, hb = _entry_ln(x.reshape(n, d), ln_emb_g, ln_emb_b)

    o_qk = 0
    o_v = 2 * DA_WIDTH
    o_hy = 3 * DA_WIDTH
    o_lat = o_hy + 3 * HY_WIDTH
    lat_w = MLA_Q_RANK + MLA_KV_RANK + MLA_ROPE
    o_g = o_lat + lat_w

    for l in range(DEPTH):
        lambda_init = 0.8 - 0.6 * math.exp(-0.3 * l)
        wl = w_in[l]
        da_scale = DA_HEAD_DIM ** -0.5 * LOG2_E
        wt_qv = jnp.concatenate([wl[:, :DA_WIDTH] * da_scale, wl[:, o_v:o_v + DA_WIDTH]], 1).T.astype(BF16)
        w_k = wl[:, DA_WIDTH:2 * DA_WIDTH].astype(BF16)
        w_hy = wl[:, o_hy:o_hy + 3 * HY_WIDTH].astype(BF16)
        w_lat = jnp.pad(wl[:, o_lat:o_lat + lat_w], ((0, 0), (0, LANES - MLA_ROPE))).astype(BF16)
        wgs = [wl[:, o_g + i * d:o_g + (i + 1) * d].astype(BF16) for i in range(N_BRANCH)]

        qvt = _proj_t(wt_qv, hb, tabs_t, DA_WIDTH)
        k_da = _matmul_rope(hb, w_k, tabs)
        lq1, lk1, lq2, lk2 = da_lambda[l].astype(F32)
        lam = jnp.exp(jnp.sum(lq1 * lk1)) - jnp.exp(jnp.sum(lq2 * lk2)) + lambda_init
        lam2 = jnp.stack([lam, jnp.asarray(1.0 - lambda_init, F32)]).astype(F32)
        o_a = _attention_t(lam2, qvt, k_da, qvt, DA_HEADS, da_subln_g[l].reshape(-1, 1), 2, batch, seq,
                           DA_HEADS, 2 * DA_HEAD_DIM, 2 * DA_HEAD_DIM, min(seq, 1024), min(seq, 1024),
                           "diff_attention")

        u_hy = _matmul(hb, w_hy, F32, 512, 1024, name="hyena_in_proj")
        uc = _short_conv(u_hy, hy_conv_w[l], hy_conv_b[l], batch, seq)
        hid = _filter_hidden(z_pad, hy_f_w1[l], hy_f_b1[l], hy_f_w2[l], hy_f_b2[l],
                             hy_f_w3[l], hy_f_b3[l], hy_f_freq[l])
        wout_pad = jnp.zeros((LANES, 2 * HY_ORDER * HY_WIDTH), F32).at[:HY_FILTER_WIDTH].set(hy_f_wout[l])
        taps_a, taps_b = _filter_taps(hid, wout_pad, decay, seg)
        spec_a = _filter_spec(dft_fwd, taps_a, nb, seg)
        spec_b = _filter_spec(dft_fwd, taps_b, nb, seg)
        w1, w2 = _dft_fwd_mul(fc, fs, uc, 0, spec_a, spec_b, 0, batch, seq, nb)
        z1 = _dft_inv_gate(fct, fst, w1, w2, uc, 0, uc, 1, hy_skip[l, 0], F32, batch * nb, seg)
        w1, w2 = _dft_fwd_mul(fc, fs, z1, 0, spec_a, spec_b, 1, batch, seq, nb)
        o_b = _dft_inv_gate(fct, fst, w1, w2, z1, 0, uc, 2, hy_skip[l, 1], BF16, batch * nb, seg)

        lat = _matmul(hb, w_lat, F32, 512, w_lat.shape[1], name="mla_latent_proj")
        wq = mla_w_uq[l].reshape(MLA_Q_RANK, MLA_HEADS, MLA_NOPE + MLA_ROPE)
        wq = jnp.pad(wq, ((0, 0), (0, 0), (0, MLA_QK_PAD - MLA_NOPE - MLA_ROPE)))
        wqt = wq.reshape(MLA_Q_RANK, MLA_HEADS * MLA_QK_PAD).T.astype(BF16)
        wkv = mla_w_ukv[l].reshape(MLA_KV_RANK, MLA_HEADS, MLA_NOPE + MLA_V)
        wk = wkv[:, :, :MLA_NOPE].reshape(MLA_KV_RANK, MLA_WIDTH).astype(BF16)
        wvt = wkv[:, :, MLA_NOPE:].reshape(MLA_KV_RANK, MLA_WIDTH).T.astype(BF16)
        mla_scale = (MLA_NOPE + MLA_ROPE) ** -0.5 * LOG2_E
        qt_c = _mla_q_t(lat, mla_q_norm_g[l], wqt, tabs_t, mla_scale)
        k_cat, vt_c = _mla_kv(lat, mla_kv_norm_g[l], wk, wvt, tabs)
        o_c = _attention_t(lam2, qt_c, k_cat, vt_c, 0, jnp.ones((MLA_V, 1), F32), 1, batch, seq,
                           MLA_HEADS, MLA_QK_PAD, MLA_V, min(seq, 2048), min(seq, 1024), "mla_attention")

        merged = _merge(hb, wgs, gate_b[l], (o_a, o_b, o_c),
                        (w_branch_a[l].astype(BF16), w_branch_b[l].astype(BF16), w_branch_c[l].astype(BF16)))
        h, hb = _matmul_res_ln(merged, w_out[l].astype(BF16), h, ln1_g[l], ln1_b[l], tk=d, name="out_proj_ln")

        ff = _matmul(hb, mlp_w1[l].astype(BF16), BF16, 1024, 1024, square_relu=True, name="mlp_up")
        h, hb = _matmul_res_ln(ff, mlp_w2[l].astype(BF16), h, ln2_g[l], ln2_b[l], name="mlp_down_ln")

    return h.reshape(batch, seq, d)
```

```python
import functools
import math

import jax
import jax.numpy as jnp
from jax import lax
from jax.experimental import pallas as pl
from jax.experimental.pallas import tpu as pltpu

D_MODEL = 2048
DEPTH = 4
ROPE_DIM = 64
ROPE_THETA = 10000.0
DA_HEADS = 8
DA_HEAD_DIM = 64
DA_WIDTH = DA_HEADS * 2 * DA_HEAD_DIM
HY_WIDTH = D_MODEL // 2
HY_ORDER = 2
HY_EMB = 33
HY_FILTER_WIDTH = 64
HY_DECAY_TARGET = 1e-2
HY_FAST_DECAY = 0.3
HY_SLOW_DECAY = 1.5
MLA_HEADS = 8
MLA_NOPE = 128
MLA_ROPE = 64
MLA_V = 128
MLA_Q_RANK = D_MODEL // 4
MLA_KV_RANK = D_MODEL // 8
MLA_WIDTH = MLA_HEADS * MLA_V
N_BRANCH = 3
D_FF = 4 * D_MODEL
DEEPNORM_ALPHA = (2 * DEPTH) ** 0.25
LN_EPS = 1e-5
RMS_EPS = 1e-6

LANES = 128
BF16_SUBLANES = 16
MLA_QK_PAD = 2 * LANES
LOG2_E = math.log2(math.e)
HY_SEGMENT = 1024
VMEM_LIMIT_BYTES = 56 * 1024 * 1024

BF16 = jnp.bfloat16
F32 = jnp.float32


def _params(*semantics):
    return pltpu.CompilerParams(dimension_semantics=semantics,
                                vmem_limit_bytes=VMEM_LIMIT_BYTES)


def _dot(a, b):
    return jnp.dot(a, b, preferred_element_type=F32)


def _dot_nt(a, b):
    return lax.dot_general(a, b, (((1,), (1,)), ((), ())), preferred_element_type=F32)


def _layer_norm(v, g, b):
    mu = jnp.mean(v, -1, keepdims=True)
    d = v - mu
    var = jnp.mean(d * d, -1, keepdims=True)
    return d * lax.rsqrt(var + LN_EPS) * g + b


def _rope_lanes(x, c, sa, sb):
    return x * c + pltpu.roll(x, LANES - 32, 1) * sa + pltpu.roll(x, 32, 1) * sb


def _ln_kernel(x_ref, g_ref, b_ref, o_ref, ob_ref):
    y = _layer_norm(x_ref[...], g_ref[...], b_ref[...])
    o_ref[...] = y
    ob_ref[...] = y.astype(BF16)


def _entry_ln(x, g, b, tm=512):
    n, d = x.shape
    row = pl.BlockSpec((tm, d), lambda i: (i, 0))
    vec = pl.BlockSpec((1, d), lambda i: (0, 0))
    return pl.pallas_call(
        _ln_kernel, grid=(n // tm,), in_specs=[row, vec, vec], out_specs=[row, row],
        out_shape=[jax.ShapeDtypeStruct((n, d), F32), jax.ShapeDtypeStruct((n, d), BF16)],
        compiler_params=_params("parallel"), name="entry_ln")(x, g.reshape(1, d), b.reshape(1, d))


def _mm_kernel(a_ref, b_ref, o_ref, *, square_relu):
    acc = _dot(a_ref[...], b_ref[...])
    if square_relu:
        acc = jnp.square(jnp.maximum(acc, 0.0))
    o_ref[...] = acc.astype(o_ref.dtype)


def _matmul(a, b, out_dtype, tm, tn, square_relu=False, name="matmul"):
    m, k = a.shape
    _, n = b.shape
    return pl.pallas_call(
        functools.partial(_mm_kernel, square_relu=square_relu),
        grid=(m // tm, n // tn),
        in_specs=[pl.BlockSpec((tm, k), lambda i, j: (i, 0)),
                  pl.BlockSpec((k, tn), lambda i, j: (0, j))],
        out_specs=pl.BlockSpec((tm, tn), lambda i, j: (i, j)),
        out_shape=jax.ShapeDtypeStruct((m, n), out_dtype),
        compiler_params=_params("parallel", "parallel"), name=name)(a, b)


def _mm_rope_kernel(a_ref, b_ref, c_ref, sa_ref, sb_ref, o_ref):
    acc = _dot(a_ref[...], b_ref[...])
    c, sa, sb = c_ref[...], sa_ref[...], sb_ref[...]
    for j in range(acc.shape[1] // LANES):
        sl = slice(j * LANES, (j + 1) * LANES)
        o_ref[:, sl] = _rope_lanes(acc[:, sl], c, sa, sb).astype(o_ref.dtype)


def _matmul_rope(a, b, tabs, tm=512):
    m, k = a.shape
    _, n = b.shape
    tab = pl.BlockSpec((tm, LANES), lambda i: (i, 0))
    return pl.pallas_call(
        _mm_rope_kernel, grid=(m // tm,),
        in_specs=[pl.BlockSpec((tm, k), lambda i: (i, 0)),
                  pl.BlockSpec((k, n), lambda i: (0, 0)), tab, tab, tab],
        out_specs=pl.BlockSpec((tm, n), lambda i: (i, 0)),
        out_shape=jax.ShapeDtypeStruct((m, n), BF16),
        compiler_params=_params("parallel"), name="qk_rope_proj")(a, b, *tabs)


def _rope_rows(x1, x2, cos_t, sin_t):
    return x1 * cos_t - x2 * sin_t, x2 * cos_t + x1 * sin_t


def _proj_t_kernel(wt_ref, h_ref, cos_ref, sin_ref, o_ref, *, rope_rows):
    acc = _dot_nt(wt_ref[...], h_ref[...])
    cos_t, sin_t = cos_ref[...], sin_ref[...]
    half = ROPE_DIM // 2
    for g in range(rope_rows // ROPE_DIM):
        r = g * ROPE_DIM
        y1, y2 = _rope_rows(acc[r:r + half], acc[r + half:r + ROPE_DIM], cos_t, sin_t)
        o_ref[r:r + half, :] = y1.astype(o_ref.dtype)
        o_ref[r + half:r + ROPE_DIM, :] = y2.astype(o_ref.dtype)
    o_ref[rope_rows:, :] = acc[rope_rows:].astype(o_ref.dtype)


def _proj_t(wt, hb, tabs_t, rope_rows, tm=512):
    f, k = wt.shape
    n = hb.shape[0]
    tab = pl.BlockSpec((ROPE_DIM // 2, tm), lambda i: (0, i))
    return pl.pallas_call(
        functools.partial(_proj_t_kernel, rope_rows=rope_rows), grid=(n // tm,),
        in_specs=[pl.BlockSpec((f, k), lambda i: (0, 0)),
                  pl.BlockSpec((tm, k), lambda i: (i, 0)), tab, tab],
        out_specs=pl.BlockSpec((f, tm), lambda i: (0, i)),
        out_shape=jax.ShapeDtypeStruct((f, n), BF16),
        compiler_params=_params("parallel"), name="da_qv_proj_t")(wt, hb, *tabs_t)


def _attn_t_kernel(lam_ref, qt_ref, k_ref, vt_ref, g_ref, o_ref, *, n_maps, tk, nk):
    qt = qt_ref[...]
    tq = qt.shape[1]
    if n_maps == 2:
        row = lax.broadcasted_iota(jnp.int32, qt.shape, 0)
        zero = jnp.zeros_like(qt)
        split = qt.shape[0] // 2
        qcat = jnp.concatenate([jnp.where(row < split, qt, zero), jnp.where(row >= split, qt, zero)], axis=1)
    else:
        qcat = qt
    w = qcat.shape[1]
    dv = vt_ref.shape[0]
    ones = jnp.ones((BF16_SUBLANES, tk), BF16)
    m = jnp.full((1, w), -jnp.inf, F32)
    acc = jnp.zeros((dv + BF16_SUBLANES, w), F32)
    for j in range(nk):
        kj = k_ref[j * tk:(j + 1) * tk, :]
        vj = jnp.concatenate([vt_ref[:, j * tk:(j + 1) * tk], ones], axis=0)
        s = _dot(kj, qcat)
        m_new = jnp.maximum(m, jnp.max(s, axis=0, keepdims=True))
        alpha = jnp.exp2(m - m_new)
        p = jnp.exp2(s - m_new).astype(BF16)
        acc = acc * alpha + _dot(vj, p)
        m = m_new
    o = acc[:dv] / acc[dv:dv + 1]
    if n_maps == 2:
        o = o[:, :tq] - lam_ref[0] * o[:, tq:]
        ms = jnp.mean(o * o, axis=0, keepdims=True)
        o = o * lax.rsqrt(ms + LN_EPS) * g_ref[...] * lam_ref[1]
    o_ref[...] = o.T.astype(o_ref.dtype)


def _attention_t(lam2, qt, k, vt, v_row_blk, g, n_maps, batch, seq, heads, dqk, dv, tq, tk, name):
    nq, nk = seq // tq, seq // tk
    return pl.pallas_call(
        functools.partial(_attn_t_kernel, n_maps=n_maps, tk=tk, nk=nk),
        grid=(batch, heads, nq),
        in_specs=[pl.BlockSpec(memory_space=pltpu.SMEM),
                  pl.BlockSpec((dqk, tq), lambda b, h, qi: (h, b * nq + qi)),
                  pl.BlockSpec((seq, dqk), lambda b, h, qi: (b, h)),
                  pl.BlockSpec((dv, seq), lambda b, h, qi: (v_row_blk + h, b)),
                  pl.BlockSpec((dv, 1), lambda b, h, qi: (0, 0))],
        out_specs=pl.BlockSpec((tq, dv), lambda b, h, qi: (b * nq + qi, h)),
        out_shape=jax.ShapeDtypeStruct((batch * seq, heads * dv), BF16),
        compiler_params=_params("parallel", "parallel", "parallel"),
        name=name)(lam2, qt, k, vt, g)


def _rms_norm(x, g):
    return x * lax.rsqrt(jnp.mean(x * x, -1, keepdims=True) + RMS_EPS) * g


def _mla_q_t_kernel(cq_ref, g_ref, wt_ref, cos_ref, sin_ref, o_ref, *, scale):
    xn = _rms_norm(cq_ref[...], g_ref[...]).astype(BF16)
    acc = _dot_nt(wt_ref[...], xn) * scale
    cos_t, sin_t = cos_ref[...], sin_ref[...]
    half = MLA_ROPE // 2
    for h in range(MLA_HEADS):
        base = h * MLA_QK_PAD
        r = base + MLA_NOPE
        o_ref[base:r, :] = acc[base:r].astype(o_ref.dtype)
        y1, y2 = _rope_rows(acc[r:r + half], acc[r + half:r + MLA_ROPE], cos_t, sin_t)
        o_ref[r:r + half, :] = y1.astype(o_ref.dtype)
        o_ref[r + half:r + MLA_ROPE, :] = y2.astype(o_ref.dtype)
        o_ref[r + MLA_ROPE:base + MLA_QK_PAD, :] = acc[r + MLA_ROPE:base + MLA_QK_PAD].astype(o_ref.dtype)


def _mla_q_t(lat, g, wt, tabs_t, scale, tm=512):
    n = lat.shape[0]
    tab = pl.BlockSpec((MLA_ROPE // 2, tm), lambda i: (0, i))
    wd = MLA_HEADS * MLA_QK_PAD
    return pl.pallas_call(
        functools.partial(_mla_q_t_kernel, scale=scale), grid=(n // tm,),
        in_specs=[pl.BlockSpec((tm, MLA_Q_RANK), lambda i: (i, 0)),
                  pl.BlockSpec((1, MLA_Q_RANK), lambda i: (0, 0)),
                  pl.BlockSpec((wd, MLA_Q_RANK), lambda i: (0, 0)), tab, tab],
        out_specs=pl.BlockSpec((wd, tm), lambda i: (0, i)),
        out_shape=jax.ShapeDtypeStruct((wd, n), BF16),
        compiler_params=_params("parallel"), name="mla_q_up_t")(lat, g.reshape(1, -1), wt, *tabs_t)


def _mla_kv_kernel(ckv_ref, kr_ref, g_ref, wk_ref, wvt_ref, c_ref, sa_ref, sb_ref, k_ref, vt_ref):
    xn = _rms_norm(ckv_ref[...], g_ref[...]).astype(BF16)
    kn = _dot(xn, wk_ref[...])
    vt_ref[...] = _dot_nt(wvt_ref[...], xn).astype(vt_ref.dtype)
    kr = _rope_lanes(kr_ref[...], c_ref[...], sa_ref[...], sb_ref[...]).astype(k_ref.dtype)
    for h in range(MLA_HEADS):
        base = h * MLA_QK_PAD
        k_ref[:, base:base + LANES] = kn[:, h * MLA_NOPE:(h + 1) * MLA_NOPE].astype(k_ref.dtype)
        k_ref[:, base + LANES:base + 2 * LANES] = kr


def _mla_kv(lat, g, wk, wvt, tabs, tm=512):
    n = lat.shape[0]
    tab = pl.BlockSpec((tm, LANES), lambda i: (i, 0))
    wd = MLA_HEADS * MLA_QK_PAD
    ckv_blk = MLA_Q_RANK // MLA_KV_RANK
    kr_blk = (MLA_Q_RANK + MLA_KV_RANK) // LANES
    return pl.pallas_call(
        _mla_kv_kernel, grid=(n // tm,),
        in_specs=[pl.BlockSpec((tm, MLA_KV_RANK), lambda i: (i, ckv_blk)),
                  pl.BlockSpec((tm, LANES), lambda i: (i, kr_blk)),
                  pl.BlockSpec((1, MLA_KV_RANK), lambda i: (0, 0)),
                  pl.BlockSpec((MLA_KV_RANK, MLA_WIDTH), lambda i: (0, 0)),
                  pl.BlockSpec((MLA_WIDTH, MLA_KV_RANK), lambda i: (0, 0)), tab, tab, tab],
        out_specs=[pl.BlockSpec((tm, wd), lambda i: (i, 0)),
                   pl.BlockSpec((MLA_WIDTH, tm), lambda i: (0, i))],
        out_shape=[jax.ShapeDtypeStruct((n, wd), BF16), jax.ShapeDtypeStruct((MLA_WIDTH, n), BF16)],
        compiler_params=_params("parallel"), name="mla_kv_up")(lat, lat, g.reshape(1, -1), wk, wvt, *tabs)


def _proj_conv_kernel(prev_ref, a_ref, next_ref, b_ref, w_ref, bias_ref, o_ref, *, tiles_per_seq):
    halo = prev_ref.shape[0]
    tm = a_ref.shape[0]
    ext = jnp.concatenate([prev_ref[...], a_ref[...], next_ref[...]], axis=0)
    u = _dot(ext, b_ref[...])
    rows = u.shape[0]
    prev = pltpu.roll(u, 1, 0)[halo:halo + tm]
    nxt = pltpu.roll(u, rows - 1, 0)[halo:halo + tm]
    tile = pl.program_id(0) % tiles_per_seq
    row = lax.broadcasted_iota(jnp.int32, prev.shape, 0)
    prev = jnp.where((row == 0) & (tile == 0), 0.0, prev)
    nxt = jnp.where((row == tm - 1) & (tile == tiles_per_seq - 1), 0.0, nxt)
    w = w_ref[...]
    o_ref[...] = prev * w[0:1] + u[halo:halo + tm] * w[1:2] + nxt * w[2:3] + bias_ref[...]


def _proj_short_conv(hb, wgt, conv_w, conv_b, seq, tm=512, tn=1024):
    n, k = hb.shape
    c = wgt.shape[1]
    halo = BF16_SUBLANES
    per = tm // halo
    last = n // halo - 1
    vec = lambda r: pl.BlockSpec((r, tn), lambda i, j: (0, j))
    return pl.pallas_call(
        functools.partial(_proj_conv_kernel, tiles_per_seq=seq // tm), grid=(n // tm, c // tn),
        in_specs=[pl.BlockSpec((halo, k), lambda i, j: (jnp.maximum(i * per - 1, 0), 0)),
                  pl.BlockSpec((tm, k), lambda i, j: (i, 0)),
                  pl.BlockSpec((halo, k), lambda i, j: (jnp.minimum((i + 1) * per, last), 0)),
                  pl.BlockSpec((k, tn), lambda i, j: (0, j)), vec(3), vec(1)],
        out_specs=pl.BlockSpec((tm, tn), lambda i, j: (i, j)),
        out_shape=jax.ShapeDtypeStruct((n, c), F32),
        compiler_params=_params("parallel", "parallel"),
        name="hyena_in_proj_conv")(hb, hb, hb, wgt, conv_w, conv_b.reshape(1, c))


def _filter_hidden_kernel(z_ref, w1_ref, b1_ref, w2_ref, b2_ref, w3_ref, b3_ref, fr_ref, o_ref):
    hp = lax.Precision.HIGHEST
    fr = fr_ref[...]
    h = jnp.sin(fr * (jnp.dot(z_ref[...], w1_ref[...], precision=hp, preferred_element_type=F32) + b1_ref[...]))
    h = jnp.sin(fr * (jnp.dot(h, w2_ref[...], precision=hp, preferred_element_type=F32) + b2_ref[...]))
    o_ref[...] = jnp.sin(fr * (jnp.dot(h, w3_ref[...], precision=hp, preferred_element_type=F32) + b3_ref[...]))


def _filter_hidden(z_pad, w1, b1, w2, b2, w3, b3, freq):
    seq = z_pad.shape[0]

    def padm(w):
        return jnp.zeros((LANES, LANES), F32).at[:w.shape[0], :w.shape[1]].set(w)

    def padv(v):
        return jnp.zeros((1, LANES), F32).at[0, :v.shape[0]].set(v)

    return pl.pallas_call(
        _filter_hidden_kernel,
        out_shape=jax.ShapeDtypeStruct((seq, LANES), F32),
        compiler_params=pltpu.CompilerParams(vmem_limit_bytes=VMEM_LIMIT_BYTES),
        name="hyena_filter_hidden")(z_pad, padm(w1), padv(b1), padm(w2), padv(b2), padm(w3), padv(b3), padv(freq))


def _filter_pm_kernel(hid_ref, wf_ref, wb_ref, dec_ref, a_ref, b_ref):
    hp = lax.Precision.HIGHEST
    hid = hid_ref[...]
    dec = dec_ref[...]
    fwd = jnp.dot(hid, wf_ref[...], precision=hp, preferred_element_type=F32) * dec
    bwd = jnp.dot(hid, wb_ref[...], precision=hp, preferred_element_type=F32) * dec
    first = lax.broadcasted_iota(jnp.int32, fwd.shape, 0) == 0
    a = jnp.where(first, fwd + bwd, fwd)
    bt = jnp.where(first, 0.0, bwd)
    norm = jnp.sum(jnp.abs(a), 0, keepdims=True) + jnp.sum(jnp.abs(bt), 0, keepdims=True)
    inv = 1.0 / norm
    seq = a.shape[0]
    lead = a_ref.shape[0] - seq
    zeros = jnp.zeros((lead, a.shape[1]), a_ref.dtype)
    a_ref[:lead, :] = zeros
    b_ref[:lead, :] = zeros
    a_ref[lead:, :] = (a * inv).astype(a_ref.dtype)
    b_ref[lead:, :] = (bt * inv).astype(b_ref.dtype)


def _filter_taps(hid, wout_pad, decay, seg, tc=256):
    seq = hid.shape[0]
    cb = HY_WIDTH // tc
    out = jax.ShapeDtypeStruct((seg + seq, HY_ORDER * HY_WIDTH), BF16)
    oblk = pl.BlockSpec((seg + seq, tc), lambda n, j: (0, n * cb + j))
    return pl.pallas_call(
        _filter_pm_kernel, grid=(HY_ORDER, cb),
        in_specs=[pl.BlockSpec((seq, LANES), lambda n, j: (0, 0)),
                  pl.BlockSpec((LANES, tc), lambda n, j: (0, 2 * n * cb + j)),
                  pl.BlockSpec((LANES, tc), lambda n, j: (0, (2 * n + 1) * cb + j)),
                  pl.BlockSpec((seq, tc), lambda n, j: (0, j))],
        out_specs=[oblk, oblk], out_shape=[out, out],
        compiler_params=_params("parallel", "parallel"), name="hyena_filter_taps")(hid, wout_pad, wout_pad, decay)


def _filter_spec_kernel(tcr_ref, tc_ref, tsr_ref, ts_ref, lo_ref, hi_ref, u_ref, v_ref):
    lo, hi = lo_ref[...], hi_ref[...]
    u_ref[0] = _dot(tcr_ref[...], lo) + _dot(tc_ref[...], hi)
    v_ref[0] = _dot(tsr_ref[...], lo) + _dot(ts_ref[...], hi)


def _filter_spec(tabs, taps, nb, seg, tf=512, tcol=512):
    fc, fs, fcr, fsr_neg = tabs
    cols = taps.shape[1]
    tblk = pl.BlockSpec((tf, seg), lambda q, fi, j: (fi, 0))
    out = jax.ShapeDtypeStruct((nb, seg, cols), F32)
    oblk = pl.BlockSpec((1, tf, tcol), lambda q, fi, j: (q, fi, j))
    return pl.pallas_call(
        _filter_spec_kernel, grid=(nb, seg // tf, cols // tcol),
        in_specs=[tblk, tblk, tblk, tblk,
                  pl.BlockSpec((seg, tcol), lambda q, fi, j: (q, j)),
                  pl.BlockSpec((seg, tcol), lambda q, fi, j: (q + 1, j))],
        out_specs=[oblk, oblk], out_shape=[out, out],
        compiler_params=_params("parallel", "parallel", "parallel"),
        name="hyena_filter_spec")(fcr, fc, fsr_neg, fs, taps, taps)


def _dft_fwd_mul_kernel(fc_ref, fs_ref, z_ref, ua_ref, va_ref, ub_ref, vb_ref, w1_ref, w2_ref, *, nb):
    seg = z_ref.shape[0] // nb
    fc, fs = fc_ref[...], fs_ref[...]
    cus, sus = [], []
    for i in range(nb):
        zi = z_ref[i * seg:(i + 1) * seg, :].astype(BF16)
        cus.append(_dot(fc, zi))
        sus.append(_dot(fs, zi))
    for j in range(nb):
        w1 = w2 = None
        for i in range(nb):
            o = j - i
            cu, su = cus[i], sus[i]
            if o > 0:
                p, q = ua_ref[o], va_ref[o]
                t1, t2 = cu * p - su * q, cu * q + su * p
            elif o < 0:
                p, q = ub_ref[-o], vb_ref[-o]
                t1, t2 = cu * p + su * q, su * p - cu * q
            else:
                p, q = ua_ref[0] + ub_ref[0], va_ref[0] - vb_ref[0]
                t1, t2 = cu * p - su * q, cu * q + su * p
            w1 = t1 if w1 is None else w1 + t1
            w2 = t2 if w2 is None else w2 + t2
        w1_ref[j] = w1.astype(w1_ref.dtype)
        w2_ref[j] = w2.astype(w2_ref.dtype)


def _dft_fwd_mul(fc, fs, z, z_col_blk, spec_a, spec_b, order, batch, seq, nb, tf=512, tc=256):
    cb = HY_WIDTH // tc
    seg = seq // nb
    ua, va = spec_a
    ub, vb = spec_b
    out = jax.ShapeDtypeStruct((batch * nb, seg, HY_WIDTH), BF16)
    oblk = pl.BlockSpec((nb, tf, tc), lambda fi, j, b: (b, fi, j))
    kblk = pl.BlockSpec((nb, tf, tc), lambda fi, j, b: (0, fi, order * cb + j))
    fblk = pl.BlockSpec((tf, seg), lambda fi, j, b: (fi, 0))
    return pl.pallas_call(
        functools.partial(_dft_fwd_mul_kernel, nb=nb), grid=(seg // tf, cb, batch),
        in_specs=[fblk, fblk,
                  pl.BlockSpec((seq, tc), lambda fi, j, b: (b, z_col_blk * cb + j)),
                  kblk, kblk, kblk, kblk],
        out_specs=[oblk, oblk], out_shape=[out, out],
        compiler_params=_params("parallel", "parallel", "parallel"),
        name="hyena_dft_fwd")(fc, fs, z, ua, va, ub, vb)


def _dft_inv_gate_kernel(fct_ref, fst_ref, w1_ref, w2_ref, z_ref, gate_ref, skip_ref, o_ref, *, inv_len):
    y = (_dot(fct_ref[...], w1_ref[0]) + _dot(fst_ref[...], w2_ref[0])) * inv_len
    o_ref[...] = (gate_ref[...] * (y + z_ref[...] * skip_ref[...])).astype(o_ref.dtype)


def _dft_inv_gate(fct, fst, w1, w2, z, z_col_blk, gate, gate_col_blk, skip, out_dtype,
                  batch, seq, tc=512):
    tt = min(seq, 1024)
    cb = HY_WIDTH // tc
    nt = seq // tt
    fblk = pl.BlockSpec((tt, seq), lambda ti, b, j: (ti, 0))
    wblk = pl.BlockSpec((1, seq, tc), lambda ti, b, j: (b, 0, j))
    return pl.pallas_call(
        functools.partial(_dft_inv_gate_kernel, inv_len=1.0 / seq), grid=(nt, batch, cb),
        in_specs=[fblk, fblk, wblk, wblk,
                  pl.BlockSpec((tt, tc), lambda ti, b, j: (b * nt + ti, z_col_blk * cb + j)),
                  pl.BlockSpec((tt, tc), lambda ti, b, j: (b * nt + ti, gate_col_blk * cb + j)),
                  pl.BlockSpec((1, tc), lambda ti, b, j: (0, j))],
        out_specs=pl.BlockSpec((tt, tc), lambda ti, b, j: (b * nt + ti, j)),
        out_shape=jax.ShapeDtypeStruct((batch * seq, HY_WIDTH), out_dtype),
        compiler_params=_params("parallel", "parallel", "parallel"),
        name="hyena_dft_inv")(fct, fst, w1, w2, z, gate, skip.reshape(1, HY_WIDTH))


def _merge_kernel(h_ref, wga_ref, wgb_ref, wgc_ref, gb_ref, oa_ref, ob_ref, oc_ref,
                  wa_ref, wb_ref, wc_ref, o_ref):
    hb = h_ref[...]
    acc = None
    for i, (wg, o, w) in enumerate(((wga_ref, oa_ref, wa_ref), (wgb_ref, ob_ref, wb_ref),
                                    (wgc_ref, oc_ref, wc_ref))):
        g = _dot(hb, wg[...]) + gb_ref[i]
        term = (1.0 / (1.0 + jnp.exp(-g))) * _dot(o[...], w[...])
        acc = term if acc is None else acc + term
    o_ref[...] = acc.astype(o_ref.dtype)


def _merge(hb, wgs, gate_b, outs, wbs, tm=512, tn=512):
    n, d = hb.shape
    row = lambda w: pl.BlockSpec((tm, w), lambda i, j: (i, 0))
    col = lambda k: pl.BlockSpec((k, tn), lambda i, j: (0, j))
    return pl.pallas_call(
        _merge_kernel, grid=(n // tm, d // tn),
        in_specs=[row(d), col(d), col(d), col(d),
                  pl.BlockSpec((N_BRANCH, 1, tn), lambda i, j: (0, 0, j)),
                  row(DA_WIDTH), row(HY_WIDTH), row(MLA_WIDTH),
                  col(DA_WIDTH), col(HY_WIDTH), col(MLA_WIDTH)],
        out_specs=pl.BlockSpec((tm, tn), lambda i, j: (i, j)),
        out_shape=jax.ShapeDtypeStruct((n, d), BF16),
        compiler_params=_params("parallel", "parallel"),
        name="gated_merge")(hb, *wgs, gate_b.reshape(N_BRANCH, 1, d), *outs, *wbs)


def _mm_res_ln_kernel(a_ref, b_ref, h_ref, g_ref, beta_ref, o_ref, ob_ref, acc_ref, *, nk):
    ki = pl.program_id(1)

    @pl.when(ki == 0)
    def _():
        acc_ref[...] = jnp.zeros(acc_ref.shape, F32)

    acc_ref[...] += _dot(a_ref[...], b_ref[...])

    @pl.when(ki == nk - 1)
    def _():
        y = _layer_norm(DEEPNORM_ALPHA * h_ref[...] + acc_ref[...], g_ref[...], beta_ref[...])
        o_ref[...] = y
        ob_ref[...] = y.astype(BF16)


def _matmul_res_ln(a, b, h, g, beta, tm=512, tk=1024, name="matmul_res_ln"):
    n, k = a.shape
    d = b.shape[1]
    nk = k // tk
    row = pl.BlockSpec((tm, d), lambda i, kk: (i, 0))
    vec = pl.BlockSpec((1, d), lambda i, kk: (0, 0))
    return pl.pallas_call(
        functools.partial(_mm_res_ln_kernel, nk=nk), grid=(n // tm, nk),
        in_specs=[pl.BlockSpec((tm, tk), lambda i, kk: (i, kk)),
                  pl.BlockSpec((tk, d), lambda i, kk: (kk, 0)), row, vec, vec],
        out_specs=[row, row],
        out_shape=[jax.ShapeDtypeStruct((n, d), F32), jax.ShapeDtypeStruct((n, d), BF16)],
        scratch_shapes=[pltpu.VMEM((tm, d), F32)],
        compiler_params=_params("parallel", "arbitrary"), name=name)(a, b, h, g.reshape(1, d), beta.reshape(1, d))


def _rope_tables(positions):
    inv = ROPE_THETA ** (-jnp.arange(0, ROPE_DIM, 2, dtype=F32) / ROPE_DIM)
    ang = positions.astype(F32).reshape(-1, 1) * inv
    cos, sin = jnp.cos(ang), jnp.sin(ang)
    zero = jnp.zeros_like(sin)
    c = jnp.concatenate([cos, cos, cos, cos], -1)
    sa = jnp.concatenate([-sin, zero, -sin, zero], -1)
    sb = jnp.concatenate([zero, sin, zero, sin], -1)
    return (c, sa, sb), (cos.T, sin.T)


def _hyena_features(seq_len):
    t = jnp.linspace(0.0, 1.0, seq_len, dtype=F32)[:, None]
    bands = (HY_EMB - 1) // 2
    w = 2.0 * math.pi * jnp.arange(seq_len, dtype=F32)[:, None] / seq_len
    f = jnp.linspace(1e-4, bands - 1, bands, dtype=F32)[None, :]
    ang = f * w
    z = jnp.concatenate([t, jnp.cos(ang), -jnp.sin(ang)], axis=-1)
    deltas = jnp.linspace(math.log(HY_DECAY_TARGET) / HY_FAST_DECAY,
                          math.log(HY_DECAY_TARGET) / HY_SLOW_DECAY, HY_WIDTH, dtype=F32)
    decay = jnp.exp(-t * jnp.abs(deltas)[None, :])
    z_pad = jnp.zeros((seq_len, LANES), F32).at[:, :HY_EMB].set(z)
    return z_pad, decay


def _dft_tables(seg):
    f = jnp.arange(seg, dtype=jnp.int32)[:, None]
    s = jnp.arange(seg, dtype=jnp.int32)[None, :]
    scale = 2.0 * math.pi / (4 * seg)
    ang = (((2 * f + 1) * s) % (4 * seg)).astype(F32) * scale
    ang_r = (((2 * f + 1) * (seg - s)) % (4 * seg)).astype(F32) * scale
    fc = jnp.cos(ang).astype(BF16)
    fs = jnp.sin(ang).astype(BF16)
    fcr = jnp.where(s == 0, 0.0, jnp.cos(ang_r)).astype(BF16)
    fsr_neg = jnp.where(s == 0, 0.0, -jnp.sin(ang_r)).astype(BF16)
    return (fc, fs, fcr, fsr_neg), (fc.T, fs.T)


def kernel(x, positions, ln_emb_g, ln_emb_b, w_in, gate_b, da_lambda, da_subln_g, hy_conv_w, hy_conv_b, hy_f_w1, hy_f_b1, hy_f_w2, hy_f_b2, hy_f_w3, hy_f_b3, hy_f_freq, hy_f_wout, hy_skip, mla_q_norm_g, mla_kv_norm_g, mla_w_uq, mla_w_ukv, w_branch_a, w_branch_b, w_branch_c, w_out, ln1_g, ln1_b, mlp_w1, mlp_w2, ln2_g, ln2_b):
    batch, seq, d = x.shape
    n = batch * seq
    tabs, tabs_t = _rope_tables(positions)
    z_pad, decay = _hyena_features(seq)
    seg = min(seq, HY_SEGMENT)
    nb = seq // seg
    dft_fwd, (fct, fst) = _dft_tables(seg)
    fc, fs = dft_fwd[:2]

    h, hb = _entry_ln(x.reshape(n, d), ln_emb_g, ln_emb_b)

    o_qk = 0
    o_v = 2 * DA_WIDTH
    o_hy = 3 * DA_WIDTH
    o_lat = o_hy + 3 * HY_WIDTH
    lat_w = MLA_Q_RANK + MLA_KV_RANK + MLA_ROPE
    o_g = o_lat + lat_w

    for l in range(DEPTH):
        lambda_init = 0.8 - 0.6 * math.exp(-0.3 * l)
        wl = w_in[l]
        da_scale = DA_HEAD_DIM ** -0.5 * LOG2_E
        wt_qv = jnp.concatenate([wl[:, :DA_WIDTH] * da_scale, wl[:, o_v:o_v + DA_WIDTH]], 1).T.astype(BF16)
        w_k = wl[:, DA_WIDTH:2 * DA_WIDTH].astype(BF16)
        w_hy = wl[:, o_hy:o_hy + 3 * HY_WIDTH].astype(BF16)
        w_lat = jnp.pad(wl[:, o_lat:o_lat + lat_w], ((0, 0), (0, LANES - MLA_ROPE))).astype(BF16)
        wgs = [wl[:, o_g + i * d:o_g + (i + 1) * d].astype(BF16) for i in range(N_BRANCH)]

        qvt = _proj_t(wt_qv, hb, tabs_t, DA_WIDTH)
        k_da = _matmul_rope(hb, w_k, tabs)
        lq1, lk1, lq2, lk2 = da_lambda[l].astype(F32)
        lam = jnp.exp(jnp.sum(lq1 * lk1)) - jnp.exp(jnp.sum(lq2 * lk2)) + lambda_init
        lam2 = jnp.stack([lam, jnp.asarray(1.0 - lambda_init, F32)]).astype(F32)
        o_a = _attention_t(lam2, qvt, k_da, qvt, DA_HEADS, da_subln_g[l].reshape(-1, 1), 2, batch, seq,
                           DA_HEADS, 2 * DA_HEAD_DIM, 2 * DA_HEAD_DIM, min(seq, 1024), min(seq, 1024),
                           "diff_attention")

        uc = _proj_short_conv(hb, w_hy, hy_conv_w[l], hy_conv_b[l], seq)
        hid = _filter_hidden(z_pad, hy_f_w1[l], hy_f_b1[l], hy_f_w2[l], hy_f_b2[l],
                             hy_f_w3[l], hy_f_b3[l], hy_f_freq[l])
        wout_pad = jnp.zeros((LANES, 2 * HY_ORDER * HY_WIDTH), F32).at[:HY_FILTER_WIDTH].set(hy_f_wout[l])
        taps_a, taps_b = _filter_taps(hid, wout_pad, decay, seg)
        spec_a = _filter_spec(dft_fwd, taps_a, nb, seg)
        spec_b = _filter_spec(dft_fwd, taps_b, nb, seg)
        w1, w2 = _dft_fwd_mul(fc, fs, uc, 0, spec_a, spec_b, 0, batch, seq, nb)
        z1 = _dft_inv_gate(fct, fst, w1, w2, uc, 0, uc, 1, hy_skip[l, 0], F32, batch * nb, seg)
        w1, w2 = _dft_fwd_mul(fc, fs, z1, 0, spec_a, spec_b, 1, batch, seq, nb)
        o_b = _dft_inv_gate(fct, fst, w1, w2, z1, 0, uc, 2, hy_skip[l, 1], BF16, batch * nb, seg)

        lat = _matmul(hb, w_lat, F32, 512, w_lat.shape[1], name="mla_latent_proj")
        wq = mla_w_uq[l].reshape(MLA_Q_RANK, MLA_HEADS, MLA_NOPE + MLA_ROPE)
        wq = jnp.pad(wq, ((0, 0), (0, 0), (0, MLA_QK_PAD - MLA_NOPE - MLA_ROPE)))
        wqt = wq.reshape(MLA_Q_RANK, MLA_HEADS * MLA_QK_PAD).T.astype(BF16)
        wkv = mla_w_ukv[l].reshape(MLA_KV_RANK, MLA_HEADS, MLA_NOPE + MLA_V)
        wk = wkv[:, :, :MLA_NOPE].reshape(MLA_KV_RANK, MLA_WIDTH).astype(BF16)
        wvt = wkv[:, :, MLA_NOPE:].reshape(MLA_KV_RANK, MLA_WIDTH).T.astype(BF16)
        mla_scale = (MLA_NOPE + MLA_ROPE) ** -0.5 * LOG2_E
        qt_c = _mla_q_t(lat, mla_q_norm_g[l], wqt, tabs_t, mla_scale)
        k_cat, vt_c = _mla_kv(lat, mla_kv_norm_g[l], wk, wvt, tabs)
        o_c = _attention_t(lam2, qt_c, k_cat, vt_c, 0, jnp.ones((MLA_V, 1), F32), 1, batch, seq,
                           MLA_HEADS, MLA_QK_PAD, MLA_V, min(seq, 2048), min(seq, 1024), "mla_attention")

        merged = _merge(hb, wgs, gate_b[l], (o_a, o_b, o_c),
                        (w_branch_a[l].astype(BF16), w_branch_b[l].astype(BF16), w_branch_c[l].astype(BF16)))
        h, hb = _matmul_res_ln(merged, w_out[l].astype(BF16), h, ln1_g[l], ln1_b[l], tk=d, name="out_proj_ln")

        ff = _matmul(hb, mlp_w1[l].astype(BF16), BF16, 1024, 1024, square_relu=True, name="mlp_up")
        h, hb = _matmul_res_ln(ff, mlp_w2[l].astype(BF16), h, ln2_g[l], ln2_b[l], tk=2048, name="mlp_down_ln")

    return h.reshape(batch, seq, d)
```

```python
import functools
import math

import jax
import jax.numpy as jnp
from jax import lax
from jax.experimental import pallas as pl
from jax.experimental.pallas import tpu as pltpu

D_MODEL = 2048
DEPTH = 4
ROPE_DIM = 64
ROPE_THETA = 10000.0
DA_HEADS = 8
DA_HEAD_DIM = 64
DA_WIDTH = DA_HEADS * 2 * DA_HEAD_DIM
HY_WIDTH = D_MODEL // 2
HY_ORDER = 2
HY_EMB = 33
HY_FILTER_WIDTH = 64
HY_DECAY_TARGET = 1e-2
HY_FAST_DECAY = 0.3
HY_SLOW_DECAY = 1.5
MLA_HEADS = 8
MLA_NOPE = 128
MLA_ROPE = 64
MLA_V = 128
MLA_Q_RANK = D_MODEL // 4
MLA_KV_RANK = D_MODEL // 8
MLA_WIDTH = MLA_HEADS * MLA_V
N_BRANCH = 3
D_FF = 4 * D_MODEL
DEEPNORM_ALPHA = (2 * DEPTH) ** 0.25
LN_EPS = 1e-5
RMS_EPS = 1e-6

LANES = 128
BF16_SUBLANES = 16
MLA_QK_PAD = 2 * LANES
LOG2_E = math.log2(math.e)
HY_SEGMENT = 1024
VMEM_LIMIT_BYTES = 56 * 1024 * 1024

BF16 = jnp.bfloat16
F32 = jnp.float32


def _params(*semantics):
    return pltpu.CompilerParams(dimension_semantics=semantics,
                                vmem_limit_bytes=VMEM_LIMIT_BYTES)


def _dot(a, b):
    return jnp.dot(a, b, preferred_element_type=F32)


def _dot_nt(a, b):
    return lax.dot_general(a, b, (((1,), (1,)), ((), ())), preferred_element_type=F32)


def _layer_norm(v, g, b):
    mu = jnp.mean(v, -1, keepdims=True)
    d = v - mu
    var = jnp.mean(d * d, -1, keepdims=True)
    return d * lax.rsqrt(var + LN_EPS) * g + b


def _rope_lanes(x, c, sa, sb):
    return x * c + pltpu.roll(x, LANES - 32, 1) * sa + pltpu.roll(x, 32, 1) * sb


def _ln_kernel(x_ref, g_ref, b_ref, o_ref, ob_ref):
    y = _layer_norm(x_ref[...], g_ref[...], b_ref[...])
    o_ref[...] = y
    ob_ref[...] = y.astype(BF16)


def _entry_ln(x, g, b, tm=512):
    n, d = x.shape
    row = pl.BlockSpec((tm, d), lambda i: (i, 0))
    vec = pl.BlockSpec((1, d), lambda i: (0, 0))
    return pl.pallas_call(
        _ln_kernel, grid=(n // tm,), in_specs=[row, vec, vec], out_specs=[row, row],
        out_shape=[jax.ShapeDtypeStruct((n, d), F32), jax.ShapeDtypeStruct((n, d), BF16)],
        compiler_params=_params("parallel"), name="entry_ln")(x, g.reshape(1, d), b.reshape(1, d))


def _mm_kernel(a_ref, b_ref, o_ref, *, square_relu):
    acc = _dot(a_ref[...], b_ref[...])
    if square_relu:
        acc = jnp.square(jnp.maximum(acc, 0.0))
    o_ref[...] = acc.astype(o_ref.dtype)


def _matmul(a, b, out_dtype, tm, tn, square_relu=False, name="matmul"):
    m, k = a.shape
    _, n = b.shape
    return pl.pallas_call(
        functools.partial(_mm_kernel, square_relu=square_relu),
        grid=(m // tm, n // tn),
        in_specs=[pl.BlockSpec((tm, k), lambda i, j: (i, 0)),
                  pl.BlockSpec((k, tn), lambda i, j: (0, j))],
        out_specs=pl.BlockSpec((tm, tn), lambda i, j: (i, j)),
        out_shape=jax.ShapeDtypeStruct((m, n), out_dtype),
        compiler_params=_params("parallel", "parallel"), name=name)(a, b)


def _mm_rope_kernel(a_ref, b_ref, c_ref, sa_ref, sb_ref, o_ref):
    acc = _dot(a_ref[...], b_ref[...])
    c, sa, sb = c_ref[...], sa_ref[...], sb_ref[...]
    for j in range(acc.shape[1] // LANES):
        sl = slice(j * LANES, (j + 1) * LANES)
        o_ref[:, sl] = _rope_lanes(acc[:, sl], c, sa, sb).astype(o_ref.dtype)


def _matmul_rope(a, b, tabs, tm=512):
    m, k = a.shape
    _, n = b.shape
    tab = pl.BlockSpec((tm, LANES), lambda i: (i, 0))
    return pl.pallas_call(
        _mm_rope_kernel, grid=(m // tm,),
        in_specs=[pl.BlockSpec((tm, k), lambda i: (i, 0)),
                  pl.BlockSpec((k, n), lambda i: (0, 0)), tab, tab, tab],
        out_specs=pl.BlockSpec((tm, n), lambda i: (i, 0)),
        out_shape=jax.ShapeDtypeStruct((m, n), BF16),
        compiler_params=_params("parallel"), name="qk_rope_proj")(a, b, *tabs)


def _rope_rows(x1, x2, cos_t, sin_t):
    return x1 * cos_t - x2 * sin_t, x2 * cos_t + x1 * sin_t


def _proj_t_kernel(wt_ref, h_ref, cos_ref, sin_ref, o_ref, *, rope_rows):
    acc = _dot_nt(wt_ref[...], h_ref[...])
    cos_t, sin_t = cos_ref[...], sin_ref[...]
    half = ROPE_DIM // 2
    for g in range(rope_rows // ROPE_DIM):
        r = g * ROPE_DIM
        y1, y2 = _rope_rows(acc[r:r + half], acc[r + half:r + ROPE_DIM], cos_t, sin_t)
        o_ref[r:r + half, :] = y1.astype(o_ref.dtype)
        o_ref[r + half:r + ROPE_DIM, :] = y2.astype(o_ref.dtype)
    o_ref[rope_rows:, :] = acc[rope_rows:].astype(o_ref.dtype)


def _proj_t(wt, hb, tabs_t, rope_rows, tm=512):
    f, k = wt.shape
    n = hb.shape[0]
    tab = pl.BlockSpec((ROPE_DIM // 2, tm), lambda i: (0, i))
    return pl.pallas_call(
        functools.partial(_proj_t_kernel, rope_rows=rope_rows), grid=(n // tm,),
        in_specs=[pl.BlockSpec((f, k), lambda i: (0, 0)),
                  pl.BlockSpec((tm, k), lambda i: (i, 0)), tab, tab],
        out_specs=pl.BlockSpec((f, tm), lambda i: (0, i)),
        out_shape=jax.ShapeDtypeStruct((f, n), BF16),
        compiler_params=_params("parallel"), name="da_qv_proj_t")(wt, hb, *tabs_t)


def _attn_t_kernel(lam_ref, qt_ref, k_ref, vt_ref, g_ref, o_ref, *, n_maps, tk, nk):
    qt = qt_ref[...]
    tq = qt.shape[1]
    if n_maps == 2:
        row = lax.broadcasted_iota(jnp.int32, qt.shape, 0)
        zero = jnp.zeros_like(qt)
        split = qt.shape[0] // 2
        qcat = jnp.concatenate([jnp.where(row < split, qt, zero), jnp.where(row >= split, qt, zero)], axis=1)
    else:
        qcat = qt
    w = qcat.shape[1]
    dv = vt_ref.shape[0]
    ones = jnp.ones((BF16_SUBLANES, tk), BF16)
    m = jnp.full((1, w), -jnp.inf, F32)
    acc = jnp.zeros((dv + BF16_SUBLANES, w), F32)
    for j in range(nk):
        kj = k_ref[j * tk:(j + 1) * tk, :]
        vj = jnp.concatenate([vt_ref[:, j * tk:(j + 1) * tk], ones], axis=0)
        s = _dot(kj, qcat)
        m_new = jnp.maximum(m, jnp.max(s, axis=0, keepdims=True))
        alpha = jnp.exp2(m - m_new)
        p = jnp.exp2(s - m_new).astype(BF16)
        acc = acc * alpha + _dot(vj, p)
        m = m_new
    o = acc[:dv] / acc[dv:dv + 1]
    if n_maps == 2:
        o = o[:, :tq] - lam_ref[0] * o[:, tq:]
        ms = jnp.mean(o * o, axis=0, keepdims=True)
        o = o * lax.rsqrt(ms + LN_EPS) * g_ref[...] * lam_ref[1]
    o_ref[...] = o.T.astype(o_ref.dtype)


def _attention_t(lam2, qt, k, vt, v_row_blk, g, n_maps, batch, seq, heads, dqk, dv, tq, tk, name):
    nq, nk = seq // tq, seq // tk
    return pl.pallas_call(
        functools.partial(_attn_t_kernel, n_maps=n_maps, tk=tk, nk=nk),
        grid=(batch, heads, nq),
        in_specs=[pl.BlockSpec(memory_space=pltpu.SMEM),
                  pl.BlockSpec((dqk, tq), lambda b, h, qi: (h, b * nq + qi)),
                  pl.BlockSpec((seq, dqk), lambda b, h, qi: (b, h)),
                  pl.BlockSpec((dv, seq), lambda b, h, qi: (v_row_blk + h, b)),
                  pl.BlockSpec((dv, 1), lambda b, h, qi: (0, 0))],
        out_specs=pl.BlockSpec((tq, dv), lambda b, h, qi: (b * nq + qi, h)),
        out_shape=jax.ShapeDtypeStruct((batch * seq, heads * dv), BF16),
        compiler_params=_params("parallel", "parallel", "parallel"),
        name=name)(lam2, qt, k, vt, g)


def _rms_norm(x, g):
    return x * lax.rsqrt(jnp.mean(x * x, -1, keepdims=True) + RMS_EPS) * g


def _mla_q_t_kernel(cq_ref, g_ref, wt_ref, cos_ref, sin_ref, o_ref, *, scale):
    xn = _rms_norm(cq_ref[...], g_ref[...]).astype(BF16)
    acc = _dot_nt(wt_ref[...], xn) * scale
    cos_t, sin_t = cos_ref[...], sin_ref[...]
    half = MLA_ROPE // 2
    for h in range(MLA_HEADS):
        base = h * MLA_QK_PAD
        r = base + MLA_NOPE
        o_ref[base:r, :] = acc[base:r].astype(o_ref.dtype)
        y1, y2 = _rope_rows(acc[r:r + half], acc[r + half:r + MLA_ROPE], cos_t, sin_t)
        o_ref[r:r + half, :] = y1.astype(o_ref.dtype)
        o_ref[r + half:r + MLA_ROPE, :] = y2.astype(o_ref.dtype)
        o_ref[r + MLA_ROPE:base + MLA_QK_PAD, :] = acc[r + MLA_ROPE:base + MLA_QK_PAD].astype(o_ref.dtype)


def _mla_q_t(lat, g, wt, tabs_t, scale, tm=512):
    n = lat.shape[0]
    tab = pl.BlockSpec((MLA_ROPE // 2, tm), lambda i: (0, i))
    wd = MLA_HEADS * MLA_QK_PAD
    return pl.pallas_call(
        functools.partial(_mla_q_t_kernel, scale=scale), grid=(n // tm,),
        in_specs=[pl.BlockSpec((tm, MLA_Q_RANK), lambda i: (i, 0)),
                  pl.BlockSpec((1, MLA_Q_RANK), lambda i: (0, 0)),
                  pl.BlockSpec((wd, MLA_Q_RANK), lambda i: (0, 0)), tab, tab],
        out_specs=pl.BlockSpec((wd, tm), lambda i: (0, i)),
        out_shape=jax.ShapeDtypeStruct((wd, n), BF16),
        compiler_params=_params("parallel"), name="mla_q_up_t")(lat, g.reshape(1, -1), wt, *tabs_t)


def _mla_kv_kernel(ckv_ref, kr_ref, g_ref, wk_ref, wvt_ref, c_ref, sa_ref, sb_ref, k_ref, vt_ref):
    xn = _rms_norm(ckv_ref[...], g_ref[...]).astype(BF16)
    kn = _dot(xn, wk_ref[...])
    vt_ref[...] = _dot_nt(wvt_ref[...], xn).astype(vt_ref.dtype)
    kr = _rope_lanes(kr_ref[...], c_ref[...], sa_ref[...], sb_ref[...]).astype(k_ref.dtype)
    for h in range(MLA_HEADS):
        base = h * MLA_QK_PAD
        k_ref[:, base:base + LANES] = kn[:, h * MLA_NOPE:(h + 1) * MLA_NOPE].astype(k_ref.dtype)
        k_ref[:, base + LANES:base + 2 * LANES] = kr


def _mla_kv(lat, g, wk, wvt, tabs, tm=512):
    n = lat.shape[0]
    tab = pl.BlockSpec((tm, LANES), lambda i: (i, 0))
    wd = MLA_HEADS * MLA_QK_PAD
    ckv_blk = MLA_Q_RANK // MLA_KV_RANK
    kr_blk = (MLA_Q_RANK + MLA_KV_RANK) // LANES
    return pl.pallas_call(
        _mla_kv_kernel, grid=(n // tm,),
        in_specs=[pl.BlockSpec((tm, MLA_KV_RANK), lambda i: (i, ckv_blk)),
                  pl.BlockSpec((tm, LANES), lambda i: (i, kr_blk)),
                  pl.BlockSpec((1, MLA_KV_RANK), lambda i: (0, 0)),
                  pl.BlockSpec((MLA_KV_RANK, MLA_WIDTH), lambda i: (0, 0)),
                  pl.BlockSpec((MLA_WIDTH, MLA_KV_RANK), lambda i: (0, 0)), tab, tab, tab],
        out_specs=[pl.BlockSpec((tm, wd), lambda i: (i, 0)),
                   pl.BlockSpec((MLA_WIDTH, tm), lambda i: (0, i))],
        out_shape=[jax.ShapeDtypeStruct((n, wd), BF16), jax.ShapeDtypeStruct((MLA_WIDTH, n), BF16)],
        compiler_params=_params("parallel"), name="mla_kv_up")(lat, lat, g.reshape(1, -1), wk, wvt, *tabs)


def _proj_conv_kernel(prev_ref, a_ref, next_ref, b_ref, w_ref, bias_ref, o_ref, *, tiles_per_seq):
    halo = prev_ref.shape[0]
    tm = a_ref.shape[0]
    ext = jnp.concatenate([prev_ref[...], a_ref[...], next_ref[...]], axis=0)
    u = _dot(ext, b_ref[...])
    rows = u.shape[0]
    prev = pltpu.roll(u, 1, 0)[halo:halo + tm]
    nxt = pltpu.roll(u, rows - 1, 0)[halo:halo + tm]
    tile = pl.program_id(0) % tiles_per_seq
    row = lax.broadcasted_iota(jnp.int32, prev.shape, 0)
    prev = jnp.where((row == 0) & (tile == 0), 0.0, prev)
    nxt = jnp.where((row == tm - 1) & (tile == tiles_per_seq - 1), 0.0, nxt)
    w = w_ref[...]
    o_ref[...] = prev * w[0:1] + u[halo:halo + tm] * w[1:2] + nxt * w[2:3] + bias_ref[...]


def _proj_short_conv(hb, wgt, conv_w, conv_b, seq, tm=512, tn=1024):
    n, k = hb.shape
    c = wgt.shape[1]
    halo = BF16_SUBLANES
    per = tm // halo
    last = n // halo - 1
    vec = lambda r: pl.BlockSpec((r, tn), lambda i, j: (0, j))
    return pl.pallas_call(
        functools.partial(_proj_conv_kernel, tiles_per_seq=seq // tm), grid=(n // tm, c // tn),
        in_specs=[pl.BlockSpec((halo, k), lambda i, j: (jnp.maximum(i * per - 1, 0), 0)),
                  pl.BlockSpec((tm, k), lambda i, j: (i, 0)),
                  pl.BlockSpec((halo, k), lambda i, j: (jnp.minimum((i + 1) * per, last), 0)),
                  pl.BlockSpec((k, tn), lambda i, j: (0, j)), vec(3), vec(1)],
        out_specs=pl.BlockSpec((tm, tn), lambda i, j: (i, j)),
        out_shape=jax.ShapeDtypeStruct((n, c), F32),
        compiler_params=_params("parallel", "parallel"),
        name="hyena_in_proj_conv")(hb, hb, hb, wgt, conv_w, conv_b.reshape(1, c))


def _filter_hidden_kernel(z_ref, w1_ref, b1_ref, w2_ref, b2_ref, w3_ref, b3_ref, fr_ref, o_ref):
    hp = lax.Precision.HIGHEST
    fr = fr_ref[...]
    h = jnp.sin(fr * (jnp.dot(z_ref[...], w1_ref[...], precision=hp, preferred_element_type=F32) + b1_ref[...]))
    h = jnp.sin(fr * (jnp.dot(h, w2_ref[...], precision=hp, preferred_element_type=F32) + b2_ref[...]))
    o_ref[...] = jnp.sin(fr * (jnp.dot(h, w3_ref[...], precision=hp, preferred_element_type=F32) + b3_ref[...]))


def _filter_hidden(z_pad, w1, b1, w2, b2, w3, b3, freq):
    seq = z_pad.shape[0]

    def padm(w):
        return jnp.zeros((LANES, LANES), F32).at[:w.shape[0], :w.shape[1]].set(w)

    def padv(v):
        return jnp.zeros((1, LANES), F32).at[0, :v.shape[0]].set(v)

    return pl.pallas_call(
        _filter_hidden_kernel,
        out_shape=jax.ShapeDtypeStruct((seq, LANES), F32),
        compiler_params=pltpu.CompilerParams(vmem_limit_bytes=VMEM_LIMIT_BYTES),
        name="hyena_filter_hidden")(z_pad, padm(w1), padv(b1), padm(w2), padv(b2), padm(w3), padv(b3), padv(freq))


def _filter_pm_kernel(hid_ref, wf_ref, wb_ref, dec_ref, a_ref, b_ref):
    hp = lax.Precision.HIGHEST
    hid = hid_ref[...]
    dec = dec_ref[...]
    fwd = jnp.dot(hid, wf_ref[...], precision=hp, preferred_element_type=F32) * dec
    bwd = jnp.dot(hid, wb_ref[...], precision=hp, preferred_element_type=F32) * dec
    first = lax.broadcasted_iota(jnp.int32, fwd.shape, 0) == 0
    a = jnp.where(first, fwd + bwd, fwd)
    bt = jnp.where(first, 0.0, bwd)
    norm = jnp.sum(jnp.abs(a), 0, keepdims=True) + jnp.sum(jnp.abs(bt), 0, keepdims=True)
    inv = 1.0 / norm
    seq = a.shape[0]
    lead = a_ref.shape[0] - seq
    zeros = jnp.zeros((lead, a.shape[1]), a_ref.dtype)
    a_ref[:lead, :] = zeros
    b_ref[:lead, :] = zeros
    a_ref[lead:, :] = (a * inv).astype(a_ref.dtype)
    b_ref[lead:, :] = (bt * inv).astype(b_ref.dtype)


def _filter_taps(hid, wout_pad, decay, seg, tc=256):
    seq = hid.shape[0]
    cb = HY_WIDTH // tc
    out = jax.ShapeDtypeStruct((seg + seq, HY_ORDER * HY_WIDTH), BF16)
    oblk = pl.BlockSpec((seg + seq, tc), lambda n, j: (0, n * cb + j))
    return pl.pallas_call(
        _filter_pm_kernel, grid=(HY_ORDER, cb),
        in_specs=[pl.BlockSpec((seq, LANES), lambda n, j: (0, 0)),
                  pl.BlockSpec((LANES, tc), lambda n, j: (0, 2 * n * cb + j)),
                  pl.BlockSpec((LANES, tc), lambda n, j: (0, (2 * n + 1) * cb + j)),
                  pl.BlockSpec((seq, tc), lambda n, j: (0, j))],
        out_specs=[oblk, oblk], out_shape=[out, out],
        compiler_params=_params("parallel", "parallel"), name="hyena_filter_taps")(hid, wout_pad, wout_pad, decay)


def _filter_spec_kernel(tcr_ref, tc_ref, tsr_ref, ts_ref, lo_ref, hi_ref, u_ref, v_ref):
    lo, hi = lo_ref[...], hi_ref[...]
    u_ref[0] = _dot(tcr_ref[...], lo) + _dot(tc_ref[...], hi)
    v_ref[0] = _dot(tsr_ref[...], lo) + _dot(ts_ref[...], hi)


def _filter_spec(tabs, taps, nb, seg, tf=512, tcol=512):
    fc, fs, fcr, fsr_neg = tabs
    cols = taps.shape[1]
    tblk = pl.BlockSpec((tf, seg), lambda q, fi, j: (fi, 0))
    out = jax.ShapeDtypeStruct((nb, seg, cols), F32)
    oblk = pl.BlockSpec((1, tf, tcol), lambda q, fi, j: (q, fi, j))
    return pl.pallas_call(
        _filter_spec_kernel, grid=(nb, seg // tf, cols // tcol),
        in_specs=[tblk, tblk, tblk, tblk,
                  pl.BlockSpec((seg, tcol), lambda q, fi, j: (q, j)),
                  pl.BlockSpec((seg, tcol), lambda q, fi, j: (q + 1, j))],
        out_specs=[oblk, oblk], out_shape=[out, out],
        compiler_params=_params("parallel", "parallel", "parallel"),
        name="hyena_filter_spec")(fcr, fc, fsr_neg, fs, taps, taps)


def _dft_fwd_mul_kernel(fc_ref, fs_ref, z_ref, ua_ref, va_ref, ub_ref, vb_ref, w1_ref, w2_ref, *, nb):
    seg = z_ref.shape[0] // nb
    fc, fs = fc_ref[...], fs_ref[...]
    cus, sus = [], []
    for i in range(nb):
        zi = z_ref[i * seg:(i + 1) * seg, :].astype(BF16)
        cus.append(_dot(fc, zi))
        sus.append(_dot(fs, zi))
    for j in range(nb):
        w1 = w2 = None
        for i in range(nb):
            o = j - i
            cu, su = cus[i], sus[i]
            if o > 0:
                p, q = ua_ref[o], va_ref[o]
                t1, t2 = cu * p - su * q, cu * q + su * p
            elif o < 0:
                p, q = ub_ref[-o], vb_ref[-o]
                t1, t2 = cu * p + su * q, su * p - cu * q
            else:
                p, q = ua_ref[0] + ub_ref[0], va_ref[0] - vb_ref[0]
                t1, t2 = cu * p - su * q, cu * q + su * p
            w1 = t1 if w1 is None else w1 + t1
            w2 = t2 if w2 is None else w2 + t2
        w1_ref[j] = w1.astype(w1_ref.dtype)
        w2_ref[j] = w2.astype(w2_ref.dtype)


def _dft_fwd_mul(fc, fs, z, z_col_blk, spec_a, spec_b, order, batch, seq, nb, tf=512, tc=256):
    cb = HY_WIDTH // tc
    seg = seq // nb
    ua, va = spec_a
    ub, vb = spec_b
    out = jax.ShapeDtypeStruct((batch * nb, seg, HY_WIDTH), BF16)
    oblk = pl.BlockSpec((nb, tf, tc), lambda fi, j, b: (b, fi, j))
    kblk = pl.BlockSpec((nb, tf, tc), lambda fi, j, b: (0, fi, order * cb + j))
    fblk = pl.BlockSpec((tf, seg), lambda fi, j, b: (fi, 0))
    return pl.pallas_call(
        functools.partial(_dft_fwd_mul_kernel, nb=nb), grid=(seg // tf, cb, batch),
        in_specs=[fblk, fblk,
                  pl.BlockSpec((seq, tc), lambda fi, j, b: (b, z_col_blk * cb + j)),
                  kblk, kblk, kblk, kblk],
        out_specs=[oblk, oblk], out_shape=[out, out],
        compiler_params=_params("parallel", "parallel", "parallel"),
        name="hyena_dft_fwd")(fc, fs, z, ua, va, ub, vb)


def _dft_inv_gate_kernel(fct_ref, fst_ref, w1_ref, w2_ref, z_ref, gate_ref, skip_ref, o_ref, *, inv_len):
    y = (_dot(fct_ref[...], w1_ref[0]) + _dot(fst_ref[...], w2_ref[0])) * inv_len
    o_ref[...] = (gate_ref[...] * (y + z_ref[...] * skip_ref[...])).astype(o_ref.dtype)


def _dft_inv_gate(fct, fst, w1, w2, z, z_col_blk, gate, gate_col_blk, skip, out_dtype,
                  batch, seq, tc=512):
    tt = min(seq, 1024)
    cb = HY_WIDTH // tc
    nt = seq // tt
    fblk = pl.BlockSpec((tt, seq), lambda ti, b, j: (ti, 0))
    wblk = pl.BlockSpec((1, seq, tc), lambda ti, b, j: (b, 0, j))
    return pl.pallas_call(
        functools.partial(_dft_inv_gate_kernel, inv_len=1.0 / seq), grid=(nt, batch, cb),
        in_specs=[fblk, fblk, wblk, wblk,
                  pl.BlockSpec((tt, tc), lambda ti, b, j: (b * nt + ti, z_col_blk * cb + j)),
                  pl.BlockSpec((tt, tc), lambda ti, b, j: (b * nt + ti, gate_col_blk * cb + j)),
                  pl.BlockSpec((1, tc), lambda ti, b, j: (0, j))],
        out_specs=pl.BlockSpec((tt, tc), lambda ti, b, j: (b * nt + ti, j)),
        out_shape=jax.ShapeDtypeStruct((batch * seq, HY_WIDTH), out_dtype),
        compiler_params=_params("parallel", "parallel", "parallel"),
        name="hyena_dft_inv")(fct, fst, w1, w2, z, gate, skip.reshape(1, HY_WIDTH))


def _merge_kernel(h_ref, wga_ref, wgb_ref, wgc_ref, gb_ref, oa_ref, ob_ref, oc_ref,
                  wa_ref, wb_ref, wc_ref, o_ref):
    hb = h_ref[...]
    acc = None
    for i, (wg, o, w) in enumerate(((wga_ref, oa_ref, wa_ref), (wgb_ref, ob_ref, wb_ref),
                                    (wgc_ref, oc_ref, wc_ref))):
        g = _dot(hb, wg[...]) + gb_ref[i]
        term = (1.0 / (1.0 + jnp.exp(-g))) * _dot(o[...], w[...])
        acc = term if acc is None else acc + term
    o_ref[...] = acc.astype(o_ref.dtype)


def _merge(hb, wgs, gate_b, outs, wbs, tm=1024, tn=256):
    n, d = hb.shape
    row = lambda w: pl.BlockSpec((tm, w), lambda i, j: (i, 0))
    col = lambda k: pl.BlockSpec((k, tn), lambda i, j: (0, j))
    return pl.pallas_call(
        _merge_kernel, grid=(n // tm, d // tn),
        in_specs=[row(d), col(d), col(d), col(d),
                  pl.BlockSpec((N_BRANCH, 1, tn), lambda i, j: (0, 0, j)),
                  row(DA_WIDTH), row(HY_WIDTH), row(MLA_WIDTH),
                  col(DA_WIDTH), col(HY_WIDTH), col(MLA_WIDTH)],
        out_specs=pl.BlockSpec((tm, tn), lambda i, j: (i, j)),
        out_shape=jax.ShapeDtypeStruct((n, d), BF16),
        compiler_params=_params("parallel", "parallel"),
        name="gated_merge")(hb, *wgs, gate_b.reshape(N_BRANCH, 1, d), *outs, *wbs)


def _mm_res_ln_kernel(a_ref, b_ref, h_ref, g_ref, beta_ref, o_ref, ob_ref, acc_ref, *, nk):
    ki = pl.program_id(1)

    @pl.when(ki == 0)
    def _():
        acc_ref[...] = jnp.zeros(acc_ref.shape, F32)

    acc_ref[...] += _dot(a_ref[...], b_ref[...])

    @pl.when(ki == nk - 1)
    def _():
        y = _layer_norm(DEEPNORM_ALPHA * h_ref[...] + acc_ref[...], g_ref[...], beta_ref[...])
        o_ref[...] = y
        ob_ref[...] = y.astype(BF16)


def _matmul_res_ln(a, b, h, g, beta, tm=512, tk=1024, name="matmul_res_ln"):
    n, k = a.shape
    d = b.shape[1]
    nk = k // tk
    row = pl.BlockSpec((tm, d), lambda i, kk: (i, 0))
    vec = pl.BlockSpec((1, d), lambda i, kk: (0, 0))
    return pl.pallas_call(
        functools.partial(_mm_res_ln_kernel, nk=nk), grid=(n // tm, nk),
        in_specs=[pl.BlockSpec((tm, tk), lambda i, kk: (i, kk)),
                  pl.BlockSpec((tk, d), lambda i, kk: (kk, 0)), row, vec, vec],
        out_specs=[row, row],
        out_shape=[jax.ShapeDtypeStruct((n, d), F32), jax.ShapeDtypeStruct((n, d), BF16)],
        scratch_shapes=[pltpu.VMEM((tm, d), F32)],
        compiler_params=_params("parallel", "arbitrary"), name=name)(a, b, h, g.reshape(1, d), beta.reshape(1, d))


def _rope_tables(positions):
    inv = ROPE_THETA ** (-jnp.arange(0, ROPE_DIM, 2, dtype=F32) / ROPE_DIM)
    ang = positions.astype(F32).reshape(-1, 1) * inv
    cos, sin = jnp.cos(ang), jnp.sin(ang)
    zero = jnp.zeros_like(sin)
    c = jnp.concatenate([cos, cos, cos, cos], -1)
    sa = jnp.concatenate([-sin, zero, -sin, zero], -1)
    sb = jnp.concatenate([zero, sin, zero, sin], -1)
    return (c, sa, sb), (cos.T, sin.T)


def _hyena_features(seq_len):
    t = jnp.linspace(0.0, 1.0, seq_len, dtype=F32)[:, None]
    bands = (HY_EMB - 1) // 2
    w = 2.0 * math.pi * jnp.arange(seq_len, dtype=F32)[:, None] / seq_len
    f = jnp.linspace(1e-4, bands - 1, bands, dtype=F32)[None, :]
    ang = f * w
    z = jnp.concatenate([t, jnp.cos(ang), -jnp.sin(ang)], axis=-1)
    deltas = jnp.linspace(math.log(HY_DECAY_TARGET) / HY_FAST_DECAY,
                          math.log(HY_DECAY_TARGET) / HY_SLOW_DECAY, HY_WIDTH, dtype=F32)
    decay = jnp.exp(-t * jnp.abs(deltas)[None, :])
    z_pad = jnp.zeros((seq_len, LANES), F32).at[:, :HY_EMB].set(z)
    return z_pad, decay


def _dft_tables(seg):
    f = jnp.arange(seg, dtype=jnp.int32)[:, None]
    s = jnp.arange(seg, dtype=jnp.int32)[None, :]
    scale = 2.0 * math.pi / (4 * seg)
    ang = (((2 * f + 1) * s) % (4 * seg)).astype(F32) * scale
    ang_r = (((2 * f + 1) * (seg - s)) % (4 * seg)).astype(F32) * scale
    fc = jnp.cos(ang).astype(BF16)
    fs = jnp.sin(ang).astype(BF16)
    fcr = jnp.where(s == 0, 0.0, jnp.cos(ang_r)).astype(BF16)
    fsr_neg = jnp.where(s == 0, 0.0, -jnp.sin(ang_r)).astype(BF16)
    return (fc, fs, fcr, fsr_neg), (fc.T, fs.T)


def kernel(x, positions, ln_emb_g, ln_emb_b, w_in, gate_b, da_lambda, da_subln_g, hy_conv_w, hy_conv_b, hy_f_w1, hy_f_b1, hy_f_w2, hy_f_b2, hy_f_w3, hy_f_b3, hy_f_freq, hy_f_wout, hy_skip, mla_q_norm_g, mla_kv_norm_g, mla_w_uq, mla_w_ukv, w_branch_a, w_branch_b, w_branch_c, w_out, ln1_g, ln1_b, mlp_w1, mlp_w2, ln2_g, ln2_b):
    batch, seq, d = x.shape
    n = batch * seq
    tabs, tabs_t = _rope_tables(positions)
    z_pad, decay = _hyena_features(seq)
    seg = min(seq, HY_SEGMENT)
    nb = seq // seg
    dft_fwd, (fct, fst) = _dft_tables(seg)
    fc, fs = dft_fwd[:2]

    h, hb = _entry_ln(x.reshape(n, d), ln_emb_g, ln_emb_b)

    o_qk = 0
    o_v = 2 * DA_WIDTH
    o_hy = 3 * DA_WIDTH
    o_lat = o_hy + 3 * HY_WIDTH
    lat_w = MLA_Q_RANK + MLA_KV_RANK + MLA_ROPE
    o_g = o_lat + lat_w

    for l in range(DEPTH):
        lambda_init = 0.8 - 0.6 * math.exp(-0.3 * l)
        wl = w_in[l]
        da_scale = DA_HEAD_DIM ** -0.5 * LOG2_E
        wt_qv = jnp.concatenate([wl[:, :DA_WIDTH] * da_scale, wl[:, o_v:o_v + DA_WIDTH]], 1).T.astype(BF16)
        w_k = wl[:, DA_WIDTH:2 * DA_WIDTH].astype(BF16)
        w_hy = wl[:, o_hy:o_hy + 3 * HY_WIDTH].astype(BF16)
        w_lat = jnp.pad(wl[:, o_lat:o_lat + lat_w], ((0, 0), (0, LANES - MLA_ROPE))).astype(BF16)
        wgs = [wl[:, o_g + i * d:o_g + (i + 1) * d].astype(BF16) for i in range(N_BRANCH)]

        qvt = _proj_t(wt_qv, hb, tabs_t, DA_WIDTH)
        k_da = _matmul_rope(hb, w_k, tabs)
        lq1, lk1, lq2, lk2 = da_lambda[l].astype(F32)
        lam = jnp.exp(jnp.sum(lq1 * lk1)) - jnp.exp(jnp.sum(lq2 * lk2)) + lambda_init
        lam2 = jnp.stack([lam, jnp.asarray(1.0 - lambda_init, F32)]).astype(F32)
        o_a = _attention_t(lam2, qvt, k_da, qvt, DA_HEADS, da_subln_g[l].reshape(-1, 1), 2, batch, seq,
                           DA_HEADS, 2 * DA_HEAD_DIM, 2 * DA_HEAD_DIM, min(seq, 1024), min(seq, 1024),
                           "diff_attention")

        uc = _proj_short_conv(hb, w_hy, hy_conv_w[l], hy_conv_b[l], seq)
        hid = _filter_hidden(z_pad, hy_f_w1[l], hy_f_b1[l], hy_f_w2[l], hy_f_b2[l],
                             hy_f_w3[l], hy_f_b3[l], hy_f_freq[l])
        wout_pad = jnp.zeros((LANES, 2 * HY_ORDER * HY_WIDTH), F32).at[:HY_FILTER_WIDTH].set(hy_f_wout[l])
        taps_a, taps_b = _filter_taps(hid, wout_pad, decay, seg)
        spec_a = _filter_spec(dft_fwd, taps_a, nb, seg)
        spec_b = _filter_spec(dft_fwd, taps_b, nb, seg)
        w1, w2 = _dft_fwd_mul(fc, fs, uc, 0, spec_a, spec_b, 0, batch, seq, nb)
        z1 = _dft_inv_gate(fct, fst, w1, w2, uc, 0, uc, 1, hy_skip[l, 0], F32, batch * nb, seg)
        w1, w2 = _dft_fwd_mul(fc, fs, z1, 0, spec_a, spec_b, 1, batch, seq, nb)
        o_b = _dft_inv_gate(fct, fst, w1, w2, z1, 0, uc, 2, hy_skip[l, 1], BF16, batch * nb, seg)

        lat = _matmul(hb, w_lat, F32, 512, w_lat.shape[1], name="mla_latent_proj")
        wq = mla_w_uq[l].reshape(MLA_Q_RANK, MLA_HEADS, MLA_NOPE + MLA_ROPE)
        wq = jnp.pad(wq, ((0, 0), (0, 0), (0, MLA_QK_PAD - MLA_NOPE - MLA_ROPE)))
        wqt = wq.reshape(MLA_Q_RANK, MLA_HEADS * MLA_QK_PAD).T.astype(BF16)
        wkv = mla_w_ukv[l].reshape(MLA_KV_RANK, MLA_HEADS, MLA_NOPE + MLA_V)
        wk = wkv[:, :, :MLA_NOPE].reshape(MLA_KV_RANK, MLA_WIDTH).astype(BF16)
        wvt = wkv[:, :, MLA_NOPE:].reshape(MLA_KV_RANK, MLA_WIDTH).T.astype(BF16)
        mla_scale = (MLA_NOPE + MLA_ROPE) ** -0.5 * LOG2_E
        qt_c = _mla_q_t(lat, mla_q_norm_g[l], wqt, tabs_t, mla_scale)
        k_cat, vt_c = _mla_kv(lat, mla_kv_norm_g[l], wk, wvt, tabs)
        o_c = _attention_t(lam2, qt_c, k_cat, vt_c, 0, jnp.ones((MLA_V, 1), F32), 1, batch, seq,
                           MLA_HEADS, MLA_QK_PAD, MLA_V, min(seq, 2048), min(seq, 1024), "mla_attention")

        merged = _merge(hb, wgs, gate_b[l], (o_a, o_b, o_c),
                        (w_branch_a[l].astype(BF16), w_branch_b[l].astype(BF16), w_branch_c[l].astype(BF16)))
        h, hb = _matmul_res_ln(merged, w_out[l].astype(BF16), h, ln1_g[l], ln1_b[l], tk=d, name="out_proj_ln")

        ff = _matmul(hb, mlp_w1[l].astype(BF16), BF16, 1024, 2048, square_relu=True, name="mlp_up")
        h, hb = _matmul_res_ln(ff, mlp_w2[l].astype(BF16), h, ln2_g[l], ln2_b[l], tk=2048, name="mlp_down_ln")

    return h.reshape(batch, seq, d)
```

```python
import functools
import math

import jax
import jax.numpy as jnp
from jax import lax
from jax.experimental import pallas as pl
from jax.experimental.pallas import tpu as pltpu

D_MODEL = 2048
DEPTH = 4
ROPE_DIM = 64
ROPE_THETA = 10000.0
DA_HEADS = 8
DA_HEAD_DIM = 64
DA_WIDTH = DA_HEADS * 2 * DA_HEAD_DIM
HY_WIDTH = D_MODEL // 2
HY_ORDER = 2
HY_EMB = 33
HY_FILTER_WIDTH = 64
HY_DECAY_TARGET = 1e-2
HY_FAST_DECAY = 0.3
HY_SLOW_DECAY = 1.5
MLA_HEADS = 8
MLA_NOPE = 128
MLA_ROPE = 64
MLA_V = 128
MLA_Q_RANK = D_MODEL // 4
MLA_KV_RANK = D_MODEL // 8
MLA_WIDTH = MLA_HEADS * MLA_V
N_BRANCH = 3
D_FF = 4 * D_MODEL
DEEPNORM_ALPHA = (2 * DEPTH) ** 0.25
LN_EPS = 1e-5
RMS_EPS = 1e-6

LANES = 128
BF16_SUBLANES = 16
MLA_QK_PAD = 2 * LANES
LOG2_E = math.log2(math.e)
HY_SEGMENT = 1024
VMEM_LIMIT_BYTES = 56 * 1024 * 1024

BF16 = jnp.bfloat16
F32 = jnp.float32


def _params(*semantics):
    return pltpu.CompilerParams(dimension_semantics=semantics,
                                vmem_limit_bytes=VMEM_LIMIT_BYTES)


def _dot(a, b):
    return jnp.dot(a, b, preferred_element_type=F32)


def _dot_nt(a, b):
    return lax.dot_general(a, b, (((1,), (1,)), ((), ())), preferred_element_type=F32)


def _layer_norm(v, g, b):
    mu = jnp.mean(v, -1, keepdims=True)
    d = v - mu
    var = jnp.mean(d * d, -1, keepdims=True)
    return d * lax.rsqrt(var + LN_EPS) * g + b


def _rope_lanes(x, c, sa, sb):
    return x * c + pltpu.roll(x, LANES - 32, 1) * sa + pltpu.roll(x, 32, 1) * sb


def _ln_kernel(x_ref, g_ref, b_ref, o_ref, ob_ref):
    y = _layer_norm(x_ref[...], g_ref[...], b_ref[...])
    o_ref[...] = y
    ob_ref[...] = y.astype(BF16)


def _entry_ln(x, g, b, tm=512):
    n, d = x.shape
    row = pl.BlockSpec((tm, d), lambda i: (i, 0))
    vec = pl.BlockSpec((1, d), lambda i: (0, 0))
    return pl.pallas_call(
        _ln_kernel, grid=(n // tm,), in_specs=[row, vec, vec], out_specs=[row, row],
        out_shape=[jax.ShapeDtypeStruct((n, d), F32), jax.ShapeDtypeStruct((n, d), BF16)],
        compiler_params=_params("parallel"), name="entry_ln")(x, g.reshape(1, d), b.reshape(1, d))


def _mm_kernel(a_ref, b_ref, o_ref, *, square_relu):
    acc = _dot(a_ref[...], b_ref[...])
    if square_relu:
        acc = jnp.square(jnp.maximum(acc, 0.0))
    o_ref[...] = acc.astype(o_ref.dtype)


def _matmul(a, b, out_dtype, tm, tn, square_relu=False, name="matmul"):
    m, k = a.shape
    _, n = b.shape
    return pl.pallas_call(
        functools.partial(_mm_kernel, square_relu=square_relu),
        grid=(m // tm, n // tn),
        in_specs=[pl.BlockSpec((tm, k), lambda i, j: (i, 0)),
                  pl.BlockSpec((k, tn), lambda i, j: (0, j))],
        out_specs=pl.BlockSpec((tm, tn), lambda i, j: (i, j)),
        out_shape=jax.ShapeDtypeStruct((m, n), out_dtype),
        compiler_params=_params("parallel", "parallel"), name=name)(a, b)


def _mm_rope_kernel(a_ref, b_ref, c_ref, sa_ref, sb_ref, o_ref):
    acc = _dot(a_ref[...], b_ref[...])
    c, sa, sb = c_ref[...], sa_ref[...], sb_ref[...]
    for j in range(acc.shape[1] // LANES):
        sl = slice(j * LANES, (j + 1) * LANES)
        o_ref[:, sl] = _rope_lanes(acc[:, sl], c, sa, sb).astype(o_ref.dtype)


def _matmul_rope(a, b, tabs, tm=512):
    m, k = a.shape
    _, n = b.shape
    tab = pl.BlockSpec((tm, LANES), lambda i: (i, 0))
    return pl.pallas_call(
        _mm_rope_kernel, grid=(m // tm,),
        in_specs=[pl.BlockSpec((tm, k), lambda i: (i, 0)),
                  pl.BlockSpec((k, n), lambda i: (0, 0)), tab, tab, tab],
        out_specs=pl.BlockSpec((tm, n), lambda i: (i, 0)),
        out_shape=jax.ShapeDtypeStruct((m, n), BF16),
        compiler_params=_params("parallel"), name="qk_rope_proj")(a, b, *tabs)


def _rope_rows(x1, x2, cos_t, sin_t):
    return x1 * cos_t - x2 * sin_t, x2 * cos_t + x1 * sin_t


def _proj_t_kernel(wt_ref, h_ref, cos_ref, sin_ref, o_ref, *, rope_rows):
    acc = _dot_nt(wt_ref[...], h_ref[...])
    cos_t, sin_t = cos_ref[...], sin_ref[...]
    half = ROPE_DIM // 2
    for g in range(rope_rows // ROPE_DIM):
        r = g * ROPE_DIM
        y1, y2 = _rope_rows(acc[r:r + half], acc[r + half:r + ROPE_DIM], cos_t, sin_t)
        o_ref[r:r + half, :] = y1.astype(o_ref.dtype)
        o_ref[r + half:r + ROPE_DIM, :] = y2.astype(o_ref.dtype)
    o_ref[rope_rows:, :] = acc[rope_rows:].astype(o_ref.dtype)


def _proj_t(wt, hb, tabs_t, rope_rows, tm=512):
    f, k = wt.shape
    n = hb.shape[0]
    tab = pl.BlockSpec((ROPE_DIM // 2, tm), lambda i: (0, i))
    return pl.pallas_call(
        functools.partial(_proj_t_kernel, rope_rows=rope_rows), grid=(n // tm,),
        in_specs=[pl.BlockSpec((f, k), lambda i: (0, 0)),
                  pl.BlockSpec((tm, k), lambda i: (i, 0)), tab, tab],
        out_specs=pl.BlockSpec((f, tm), lambda i: (0, i)),
        out_shape=jax.ShapeDtypeStruct((f, n), BF16),
        compiler_params=_params("parallel"), name="da_qv_proj_t")(wt, hb, *tabs_t)


def _attn_t_kernel(lam_ref, qt_ref, k_ref, vt_ref, g_ref, o_ref, *, n_maps, tk, nk):
    qt = qt_ref[...]
    tq = qt.shape[1]
    if n_maps == 2:
        row = lax.broadcasted_iota(jnp.int32, qt.shape, 0)
        zero = jnp.zeros_like(qt)
        split = qt.shape[0] // 2
        qcat = jnp.concatenate([jnp.where(row < split, qt, zero), jnp.where(row >= split, qt, zero)], axis=1)
    else:
        qcat = qt
    w = qcat.shape[1]
    dv = vt_ref.shape[0]
    ones = jnp.ones((BF16_SUBLANES, tk), BF16)
    m = jnp.full((1, w), -jnp.inf, F32)
    acc = jnp.zeros((dv + BF16_SUBLANES, w), F32)
    for j in range(nk):
        kj = k_ref[j * tk:(j + 1) * tk, :]
        vj = jnp.concatenate([vt_ref[:, j * tk:(j + 1) * tk], ones], axis=0)
        s = _dot(kj, qcat)
        m_new = jnp.maximum(m, jnp.max(s, axis=0, keepdims=True))
        alpha = jnp.exp2(m - m_new)
        p = jnp.exp2(s - m_new).astype(BF16)
        acc = acc * alpha + _dot(vj, p)
        m = m_new
    o = acc[:dv] / acc[dv:dv + 1]
    if n_maps == 2:
        o = o[:, :tq] - lam_ref[0] * o[:, tq:]
        ms = jnp.mean(o * o, axis=0, keepdims=True)
        o = o * lax.rsqrt(ms + LN_EPS) * g_ref[...] * lam_ref[1]
    o_ref[...] = o.T.astype(o_ref.dtype)


def _attention_t(lam2, qt, k, vt, v_row_blk, g, n_maps, batch, seq, heads, dqk, dv, tq, tk, name):
    nq, nk = seq // tq, seq // tk
    return pl.pallas_call(
        functools.partial(_attn_t_kernel, n_maps=n_maps, tk=tk, nk=nk),
        grid=(batch, heads, nq),
        in_specs=[pl.BlockSpec(memory_space=pltpu.SMEM),
                  pl.BlockSpec((dqk, tq), lambda b, h, qi: (h, b * nq + qi)),
                  pl.BlockSpec((seq, dqk), lambda b, h, qi: (b, h)),
                  pl.BlockSpec((dv, seq), lambda b, h, qi: (v_row_blk + h, b)),
                  pl.BlockSpec((dv, 1), lambda b, h, qi: (0, 0))],
        out_specs=pl.BlockSpec((tq, dv), lambda b, h, qi: (b * nq + qi, h)),
        out_shape=jax.ShapeDtypeStruct((batch * seq, heads * dv), BF16),
        compiler_params=_params("parallel", "parallel", "parallel"),
        name=name)(lam2, qt, k, vt, g)


def _rms_norm(x, g):
    return x * lax.rsqrt(jnp.mean(x * x, -1, keepdims=True) + RMS_EPS) * g


def _mla_q_t_kernel(cq_ref, g_ref, wt_ref, cos_ref, sin_ref, o_ref, *, scale):
    xn = _rms_norm(cq_ref[...], g_ref[...]).astype(BF16)
    acc = _dot_nt(wt_ref[...], xn) * scale
    cos_t, sin_t = cos_ref[...], sin_ref[...]
    half = MLA_ROPE // 2
    for h in range(MLA_HEADS):
        base = h * MLA_QK_PAD
        r = base + MLA_NOPE
        o_ref[base:r, :] = acc[base:r].astype(o_ref.dtype)
        y1, y2 = _rope_rows(acc[r:r + half], acc[r + half:r + MLA_ROPE], cos_t, sin_t)
        o_ref[r:r + half, :] = y1.astype(o_ref.dtype)
        o_ref[r + half:r + MLA_ROPE, :] = y2.astype(o_ref.dtype)
        o_ref[r + MLA_ROPE:base + MLA_QK_PAD, :] = acc[r + MLA_ROPE:base + MLA_QK_PAD].astype(o_ref.dtype)


def _mla_q_t(lat, g, wt, tabs_t, scale, tm=512):
    n = lat.shape[0]
    tab = pl.BlockSpec((MLA_ROPE // 2, tm), lambda i: (0, i))
    wd = MLA_HEADS * MLA_QK_PAD
    return pl.pallas_call(
        functools.partial(_mla_q_t_kernel, scale=scale), grid=(n // tm,),
        in_specs=[pl.BlockSpec((tm, MLA_Q_RANK), lambda i: (i, 0)),
                  pl.BlockSpec((1, MLA_Q_RANK), lambda i: (0, 0)),
                  pl.BlockSpec((wd, MLA_Q_RANK), lambda i: (0, 0)), tab, tab],
        out_specs=pl.BlockSpec((wd, tm), lambda i: (0, i)),
        out_shape=jax.ShapeDtypeStruct((wd, n), BF16),
        compiler_params=_params("parallel"), name="mla_q_up_t")(lat, g.reshape(1, -1), wt, *tabs_t)


def _mla_kv_kernel(ckv_ref, kr_ref, g_ref, wk_ref, wvt_ref, c_ref, sa_ref, sb_ref, k_ref, vt_ref):
    xn = _rms_norm(ckv_ref[...], g_ref[...]).astype(BF16)
    kn = _dot(xn, wk_ref[...])
    vt_ref[...] = _dot_nt(wvt_ref[...], xn).astype(vt_ref.dtype)
    kr = _rope_lanes(kr_ref[...], c_ref[...], sa_ref[...], sb_ref[...]).astype(k_ref.dtype)
    for h in range(MLA_HEADS):
        base = h * MLA_QK_PAD
        k_ref[:, base:base + LANES] = kn[:, h * MLA_NOPE:(h + 1) * MLA_NOPE].astype(k_ref.dtype)
        k_ref[:, base + LANES:base + 2 * LANES] = kr


def _mla_kv(lat, g, wk, wvt, tabs, tm=512):
    n = lat.shape[0]
    tab = pl.BlockSpec((tm, LANES), lambda i: (i, 0))
    wd = MLA_HEADS * MLA_QK_PAD
    ckv_blk = MLA_Q_RANK // MLA_KV_RANK
    kr_blk = (MLA_Q_RANK + MLA_KV_RANK) // LANES
    return pl.pallas_call(
        _mla_kv_kernel, grid=(n // tm,),
        in_specs=[pl.BlockSpec((tm, MLA_KV_RANK), lambda i: (i, ckv_blk)),
                  pl.BlockSpec((tm, LANES), lambda i: (i, kr_blk)),
                  pl.BlockSpec((1, MLA_KV_RANK), lambda i: (0, 0)),
                  pl.BlockSpec((MLA_KV_RANK, MLA_WIDTH), lambda i: (0, 0)),
                  pl.BlockSpec((MLA_WIDTH, MLA_KV_RANK), lambda i: (0, 0)), tab, tab, tab],
        out_specs=[pl.BlockSpec((tm, wd), lambda i: (i, 0)),
                   pl.BlockSpec((MLA_WIDTH, tm), lambda i: (0, i))],
        out_shape=[jax.ShapeDtypeStruct((n, wd), BF16), jax.ShapeDtypeStruct((MLA_WIDTH, n), BF16)],
        compiler_params=_params("parallel"), name="mla_kv_up")(lat, lat, g.reshape(1, -1), wk, wvt, *tabs)


def _proj_conv_kernel(prev_ref, a_ref, next_ref, b_ref, w_ref, bias_ref, o_ref, *, tiles_per_seq):
    halo = prev_ref.shape[0]
    tm = a_ref.shape[0]
    ext = jnp.concatenate([prev_ref[...], a_ref[...], next_ref[...]], axis=0)
    u = _dot(ext, b_ref[...])
    rows = u.shape[0]
    prev = pltpu.roll(u, 1, 0)[halo:halo + tm]
    nxt = pltpu.roll(u, rows - 1, 0)[halo:halo + tm]
    tile = pl.program_id(0) % tiles_per_seq
    row = lax.broadcasted_iota(jnp.int32, prev.shape, 0)
    prev = jnp.where((row == 0) & (tile == 0), 0.0, prev)
    nxt = jnp.where((row == tm - 1) & (tile == tiles_per_seq - 1), 0.0, nxt)
    w = w_ref[...]
    o_ref[...] = prev * w[0:1] + u[halo:halo + tm] * w[1:2] + nxt * w[2:3] + bias_ref[...]


def _proj_short_conv(hb, wgt, conv_w, conv_b, seq, tm=512, tn=1024):
    n, k = hb.shape
    c = wgt.shape[1]
    halo = BF16_SUBLANES
    per = tm // halo
    last = n // halo - 1
    vec = lambda r: pl.BlockSpec((r, tn), lambda i, j: (0, j))
    return pl.pallas_call(
        functools.partial(_proj_conv_kernel, tiles_per_seq=seq // tm), grid=(n // tm, c // tn),
        in_specs=[pl.BlockSpec((halo, k), lambda i, j: (jnp.maximum(i * per - 1, 0), 0)),
                  pl.BlockSpec((tm, k), lambda i, j: (i, 0)),
                  pl.BlockSpec((halo, k), lambda i, j: (jnp.minimum((i + 1) * per, last), 0)),
                  pl.BlockSpec((k, tn), lambda i, j: (0, j)), vec(3), vec(1)],
        out_specs=pl.BlockSpec((tm, tn), lambda i, j: (i, j)),
        out_shape=jax.ShapeDtypeStruct((n, c), F32),
        compiler_params=_params("parallel", "parallel"),
        name="hyena_in_proj_conv")(hb, hb, hb, wgt, conv_w, conv_b.reshape(1, c))


def _filter_hidden_kernel(z_ref, w1_ref, b1_ref, w2_ref, b2_ref, w3_ref, b3_ref, fr_ref, o_ref):
    hp = lax.Precision.HIGHEST
    fr = fr_ref[...]
    h = jnp.sin(fr * (jnp.dot(z_ref[...], w1_ref[...], precision=hp, preferred_element_type=F32) + b1_ref[...]))
    h = jnp.sin(fr * (jnp.dot(h, w2_ref[...], precision=hp, preferred_element_type=F32) + b2_ref[...]))
    o_ref[...] = jnp.sin(fr * (jnp.dot(h, w3_ref[...], precision=hp, preferred_element_type=F32) + b3_ref[...]))


def _filter_hidden(z_pad, w1, b1, w2, b2, w3, b3, freq):
    seq = z_pad.shape[0]

    def padm(w):
        return jnp.zeros((LANES, LANES), F32).at[:w.shape[0], :w.shape[1]].set(w)

    def padv(v):
        return jnp.zeros((1, LANES), F32).at[0, :v.shape[0]].set(v)

    return pl.pallas_call(
        _filter_hidden_kernel,
        out_shape=jax.ShapeDtypeStruct((seq, LANES), F32),
        compiler_params=pltpu.CompilerParams(vmem_limit_bytes=VMEM_LIMIT_BYTES),
        name="hyena_filter_hidden")(z_pad, padm(w1), padv(b1), padm(w2), padv(b2), padm(w3), padv(b3), padv(freq))


def _filter_pm_kernel(hid_ref, wf_ref, wb_ref, dec_ref, a_ref, b_ref):
    hp = lax.Precision.HIGHEST
    hid = hid_ref[...]
    dec = dec_ref[...]
    fwd = jnp.dot(hid, wf_ref[...], precision=hp, preferred_element_type=F32) * dec
    bwd = jnp.dot(hid, wb_ref[...], precision=hp, preferred_element_type=F32) * dec
    first = lax.broadcasted_iota(jnp.int32, fwd.shape, 0) == 0
    a = jnp.where(first, fwd + bwd, fwd)
    bt = jnp.where(first, 0.0, bwd)
    norm = jnp.sum(jnp.abs(a), 0, keepdims=True) + jnp.sum(jnp.abs(bt), 0, keepdims=True)
    inv = 1.0 / norm
    seq = a.shape[0]
    lead = a_ref.shape[0] - seq
    zeros = jnp.zeros((lead, a.shape[1]), a_ref.dtype)
    a_ref[:lead, :] = zeros
    b_ref[:lead, :] = zeros
    a_ref[lead:, :] = (a * inv).astype(a_ref.dtype)
    b_ref[lead:, :] = (bt * inv).astype(b_ref.dtype)


def _filter_taps(hid, wout_pad, decay, seg, tc=256):
    seq = hid.shape[0]
    cb = HY_WIDTH // tc
    out = jax.ShapeDtypeStruct((seg + seq, HY_ORDER * HY_WIDTH), BF16)
    oblk = pl.BlockSpec((seg + seq, tc), lambda n, j: (0, n * cb + j))
    return pl.pallas_call(
        _filter_pm_kernel, grid=(HY_ORDER, cb),
        in_specs=[pl.BlockSpec((seq, LANES), lambda n, j: (0, 0)),
                  pl.BlockSpec((LANES, tc), lambda n, j: (0, 2 * n * cb + j)),
                  pl.BlockSpec((LANES, tc), lambda n, j: (0, (2 * n + 1) * cb + j)),
                  pl.BlockSpec((seq, tc), lambda n, j: (0, j))],
        out_specs=[oblk, oblk], out_shape=[out, out],
        compiler_params=_params("parallel", "parallel"), name="hyena_filter_taps")(hid, wout_pad, wout_pad, decay)


def _filter_spec_kernel(tcr_ref, tc_ref, tsr_ref, ts_ref, lo_ref, hi_ref, u_ref, v_ref):
    lo, hi = lo_ref[...], hi_ref[...]
    u_ref[0] = _dot(tcr_ref[...], lo) + _dot(tc_ref[...], hi)
    v_ref[0] = _dot(tsr_ref[...], lo) + _dot(ts_ref[...], hi)


def _filter_spec(tabs, taps, nb, seg, tf=512, tcol=512):
    fc, fs, fcr, fsr_neg = tabs
    cols = taps.shape[1]
    tblk = pl.BlockSpec((tf, seg), lambda q, fi, j: (fi, 0))
    out = jax.ShapeDtypeStruct((nb, seg, cols), F32)
    oblk = pl.BlockSpec((1, tf, tcol), lambda q, fi, j: (q, fi, j))
    return pl.pallas_call(
        _filter_spec_kernel, grid=(nb, seg // tf, cols // tcol),
        in_specs=[tblk, tblk, tblk, tblk,
                  pl.BlockSpec((seg, tcol), lambda q, fi, j: (q, j)),
                  pl.BlockSpec((seg, tcol), lambda q, fi, j: (q + 1, j))],
        out_specs=[oblk, oblk], out_shape=[out, out],
        compiler_params=_params("parallel", "parallel", "parallel"),
        name="hyena_filter_spec")(fcr, fc, fsr_neg, fs, taps, taps)


def _dft_fwd_mul_kernel(fc_ref, fs_ref, z_ref, ua_ref, va_ref, ub_ref, vb_ref, w1_ref, w2_ref, *, nb):
    seg = z_ref.shape[0] // nb
    fc, fs = fc_ref[...], fs_ref[...]
    cus, sus = [], []
    for i in range(nb):
        zi = z_ref[i * seg:(i + 1) * seg, :].astype(BF16)
        cus.append(_dot(fc, zi))
        sus.append(_dot(fs, zi))
    for j in range(nb):
        w1 = w2 = None
        for i in range(nb):
            o = j - i
            cu, su = cus[i], sus[i]
            if o > 0:
                p, q = ua_ref[o], va_ref[o]
                t1, t2 = cu * p - su * q, cu * q + su * p
            elif o < 0:
                p, q = ub_ref[-o], vb_ref[-o]
                t1, t2 = cu * p + su * q, su * p - cu * q
            else:
                p, q = ua_ref[0] + ub_ref[0], va_ref[0] - vb_ref[0]
                t1, t2 = cu * p - su * q, cu * q + su * p
            w1 = t1 if w1 is None else w1 + t1
            w2 = t2 if w2 is None else w2 + t2
        w1_ref[j] = w1.astype(w1_ref.dtype)
        w2_ref[j] = w2.astype(w2_ref.dtype)


def _dft_fwd_mul(fc, fs, z, z_col_blk, spec_a, spec_b, order, batch, seq, nb, tf=512, tc=256):
    cb = HY_WIDTH // tc
    seg = seq // nb
    ua, va = spec_a
    ub, vb = spec_b
    out = jax.ShapeDtypeStruct((batch * nb, seg, HY_WIDTH), BF16)
    oblk = pl.BlockSpec((nb, tf, tc), lambda fi, j, b: (b, fi, j))
    kblk = pl.BlockSpec((nb, tf, tc), lambda fi, j, b: (0, fi, order * cb + j))
    fblk = pl.BlockSpec((tf, seg), lambda fi, j, b: (fi, 0))
    return pl.pallas_call(
        functools.partial(_dft_fwd_mul_kernel, nb=nb), grid=(seg // tf, cb, batch),
        in_specs=[fblk, fblk,
                  pl.BlockSpec((seq, tc), lambda fi, j, b: (b, z_col_blk * cb + j)),
                  kblk, kblk, kblk, kblk],
        out_specs=[oblk, oblk], out_shape=[out, out],
        compiler_params=_params("parallel", "parallel", "parallel"),
        name="hyena_dft_fwd")(fc, fs, z, ua, va, ub, vb)


def _dft_inv_gate_kernel(fct_ref, fst_ref, w1_ref, w2_ref, z_ref, gate_ref, skip_ref, o_ref, *, inv_len):
    y = (_dot(fct_ref[...], w1_ref[0]) + _dot(fst_ref[...], w2_ref[0])) * inv_len
    o_ref[...] = (gate_ref[...] * (y + z_ref[...] * skip_ref[...])).astype(o_ref.dtype)


def _dft_inv_gate(fct, fst, w1, w2, z, z_col_blk, gate, gate_col_blk, skip, out_dtype,
                  batch, seq, tc=512):
    tt = min(seq, 1024)
    cb = HY_WIDTH // tc
    nt = seq // tt
    fblk = pl.BlockSpec((tt, seq), lambda ti, b, j: (ti, 0))
    wblk = pl.BlockSpec((1, seq, tc), lambda ti, b, j: (b, 0, j))
    return pl.pallas_call(
        functools.partial(_dft_inv_gate_kernel, inv_len=1.0 / seq), grid=(nt, batch, cb),
        in_specs=[fblk, fblk, wblk, wblk,
                  pl.BlockSpec((tt, tc), lambda ti, b, j: (b * nt + ti, z_col_blk * cb + j)),
                  pl.BlockSpec((tt, tc), lambda ti, b, j: (b * nt + ti, gate_col_blk * cb + j)),
                  pl.BlockSpec((1, tc), lambda ti, b, j: (0, j))],
        out_specs=pl.BlockSpec((tt, tc), lambda ti, b, j: (b * nt + ti, j)),
        out_shape=jax.ShapeDtypeStruct((batch * seq, HY_WIDTH), out_dtype),
        compiler_params=_params("parallel", "parallel", "parallel"),
        name="hyena_dft_inv")(fct, fst, w1, w2, z, gate, skip.reshape(1, HY_WIDTH))


def _merge_kernel(h_ref, wga_ref, wgb_ref, wgc_ref, gb_ref, oa_ref, ob_ref, oc_ref,
                  wa_ref, wb_ref, wc_ref, o_ref):
    hb = h_ref[...]
    acc = None
    for i, (wg, o, w) in enumerate(((wga_ref, oa_ref, wa_ref), (wgb_ref, ob_ref, wb_ref),
                                    (wgc_ref, oc_ref, wc_ref))):
        g = _dot(hb, wg[...]) + gb_ref[i]
        term = (1.0 / (1.0 + jnp.exp(-g))) * _dot(o[...], w[...])
        acc = term if acc is None else acc + term
    o_ref[...] = acc.astype(o_ref.dtype)


def _merge(hb, wgs, gate_b, outs, wbs, tm=1024, tn=256):
    n, d = hb.shape
    row = lambda w: pl.BlockSpec((tm, w), lambda i, j: (i, 0))
    col = lambda k: pl.BlockSpec((k, tn), lambda i, j: (0, j))
    return pl.pallas_call(
        _merge_kernel, grid=(n // tm, d // tn),
        in_specs=[row(d), col(d), col(d), col(d),
                  pl.BlockSpec((N_BRANCH, 1, tn), lambda i, j: (0, 0, j)),
                  row(DA_WIDTH), row(HY_WIDTH), row(MLA_WIDTH),
                  col(DA_WIDTH), col(HY_WIDTH), col(MLA_WIDTH)],
        out_specs=pl.BlockSpec((tm, tn), lambda i, j: (i, j)),
        out_shape=jax.ShapeDtypeStruct((n, d), BF16),
        compiler_params=_params("parallel", "parallel"),
        name="gated_merge")(hb, *wgs, gate_b.reshape(N_BRANCH, 1, d), *outs, *wbs)


def _mm_res_ln_kernel(a_ref, b_ref, h_ref, g_ref, beta_ref, o_ref, ob_ref, acc_ref, *, nk):
    ki = pl.program_id(1)

    @pl.when(ki == 0)
    def _():
        acc_ref[...] = jnp.zeros(acc_ref.shape, F32)

    acc_ref[...] += _dot(a_ref[...], b_ref[...])

    @pl.when(ki == nk - 1)
    def _():
        y = _layer_norm(DEEPNORM_ALPHA * h_ref[...] + acc_ref[...], g_ref[...], beta_ref[...])
        o_ref[...] = y
        ob_ref[...] = y.astype(BF16)


def _matmul_res_ln(a, b, h, g, beta, tm=512, tk=1024, name="matmul_res_ln"):
    n, k = a.shape
    d = b.shape[1]
    nk = k // tk
    row = pl.BlockSpec((tm, d), lambda i, kk: (i, 0))
    vec = pl.BlockSpec((1, d), lambda i, kk: (0, 0))
    return pl.pallas_call(
        functools.partial(_mm_res_ln_kernel, nk=nk), grid=(n // tm, nk),
        in_specs=[pl.BlockSpec((tm, tk), lambda i, kk: (i, kk)),
                  pl.BlockSpec((tk, d), lambda i, kk: (kk, 0)), row, vec, vec],
        out_specs=[row, row],
        out_shape=[jax.ShapeDtypeStruct((n, d), F32), jax.ShapeDtypeStruct((n, d), BF16)],
        scratch_shapes=[pltpu.VMEM((tm, d), F32)],
        compiler_params=_params("parallel", "arbitrary"), name=name)(a, b, h, g.reshape(1, d), beta.reshape(1, d))


def _rope_tables(positions):
    inv = ROPE_THETA ** (-jnp.arange(0, ROPE_DIM, 2, dtype=F32) / ROPE_DIM)
    ang = positions.astype(F32).reshape(-1, 1) * inv
    cos, sin = jnp.cos(ang), jnp.sin(ang)
    zero = jnp.zeros_like(sin)
    c = jnp.concatenate([cos, cos, cos, cos], -1)
    sa = jnp.concatenate([-sin, zero, -sin, zero], -1)
    sb = jnp.concatenate([zero, sin, zero, sin], -1)
    return (c, sa, sb), (cos.T, sin.T)


def _hyena_features(seq_len):
    t = jnp.linspace(0.0, 1.0, seq_len, dtype=F32)[:, None]
    bands = (HY_EMB - 1) // 2
    w = 2.0 * math.pi * jnp.arange(seq_len, dtype=F32)[:, None] / seq_len
    f = jnp.linspace(1e-4, bands - 1, bands, dtype=F32)[None, :]
    ang = f * w
    z = jnp.concatenate([t, jnp.cos(ang), -jnp.sin(ang)], axis=-1)
    deltas = jnp.linspace(math.log(HY_DECAY_TARGET) / HY_FAST_DECAY,
                          math.log(HY_DECAY_TARGET) / HY_SLOW_DECAY, HY_WIDTH, dtype=F32)
    decay = jnp.exp(-t * jnp.abs(deltas)[None, :])
    z_pad = jnp.zeros((seq_len, LANES), F32).at[:, :HY_EMB].set(z)
    return z_pad, decay


def _dft_tables(seg):
    f = jnp.arange(seg, dtype=jnp.int32)[:, None]
    s = jnp.arange(seg, dtype=jnp.int32)[None, :]
    scale = 2.0 * math.pi / (4 * seg)
    ang = (((2 * f + 1) * s) % (4 * seg)).astype(F32) * scale
    ang_r = (((2 * f + 1) * (seg - s)) % (4 * seg)).astype(F32) * scale
    fc = jnp.cos(ang).astype(BF16)
    fs = jnp.sin(ang).astype(BF16)
    fcr = jnp.where(s == 0, 0.0, jnp.cos(ang_r)).astype(BF16)
    fsr_neg = jnp.where(s == 0, 0.0, -jnp.sin(ang_r)).astype(BF16)
    return (fc, fs, fcr, fsr_neg), (fc.T, fs.T)


def kernel(x, positions, ln_emb_g, ln_emb_b, w_in, gate_b, da_lambda, da_subln_g, hy_conv_w, hy_conv_b, hy_f_w1, hy_f_b1, hy_f_w2, hy_f_b2, hy_f_w3, hy_f_b3, hy_f_freq, hy_f_wout, hy_skip, mla_q_norm_g, mla_kv_norm_g, mla_w_uq, mla_w_ukv, w_branch_a, w_branch_b, w_branch_c, w_out, ln1_g, ln1_b, mlp_w1, mlp_w2, ln2_g, ln2_b):
    batch, seq, d = x.shape
    n = batch * seq
    tabs, tabs_t = _rope_tables(positions)
    z_pad, decay = _hyena_features(seq)
    seg = min(seq, HY_SEGMENT)
    nb = seq // seg
    dft_fwd, (fct, fst) = _dft_tables(seg)
    fc, fs = dft_fwd[:2]

    h, hb = _entry_ln(x.reshape(n, d), ln_emb_g, ln_emb_b)

    o_qk = 0
    o_v = 2 * DA_WIDTH
    o_hy = 3 * DA_WIDTH
    o_lat = o_hy + 3 * HY_WIDTH
    lat_w = MLA_Q_RANK + MLA_KV_RANK + MLA_ROPE
    o_g = o_lat + lat_w

    for l in range(DEPTH):
        lambda_init = 0.8 - 0.6 * math.exp(-0.3 * l)
        wl = w_in[l]
        da_scale = DA_HEAD_DIM ** -0.5 * LOG2_E
        wt_qv = jnp.concatenate([wl[:, :DA_WIDTH] * da_scale, wl[:, o_v:o_v + DA_WIDTH]], 1).T.astype(BF16)
        w_k = wl[:, DA_WIDTH:2 * DA_WIDTH].astype(BF16)
        w_hy = wl[:, o_hy:o_hy + 3 * HY_WIDTH].astype(BF16)
        w_lat = jnp.pad(wl[:, o_lat:o_lat + lat_w], ((0, 0), (0, LANES - MLA_ROPE))).astype(BF16)
        wgs = [wl[:, o_g + i * d:o_g + (i + 1) * d].astype(BF16) for i in range(N_BRANCH)]

        qvt = _proj_t(wt_qv, hb, tabs_t, DA_WIDTH)
        k_da = _matmul_rope(hb, w_k, tabs)
        lq1, lk1, lq2, lk2 = da_lambda[l].astype(F32)
        lam = jnp.exp(jnp.sum(lq1 * lk1)) - jnp.exp(jnp.sum(lq2 * lk2)) + lambda_init
        lam2 = jnp.stack([lam, jnp.asarray(1.0 - lambda_init, F32)]).astype(F32)
        o_a = _attention_t(lam2, qvt, k_da, qvt, DA_HEADS, da_subln_g[l].reshape(-1, 1), 2, batch, seq,
                           DA_HEADS, 2 * DA_HEAD_DIM, 2 * DA_HEAD_DIM, min(seq, 2048), min(seq, 512),
                           "diff_attention")

        uc = _proj_short_conv(hb, w_hy, hy_conv_w[l], hy_conv_b[l], seq)
        hid = _filter_hidden(z_pad, hy_f_w1[l], hy_f_b1[l], hy_f_w2[l], hy_f_b2[l],
                             hy_f_w3[l], hy_f_b3[l], hy_f_freq[l])
        wout_pad = jnp.zeros((LANES, 2 * HY_ORDER * HY_WIDTH), F32).at[:HY_FILTER_WIDTH].set(hy_f_wout[l])
        taps_a, taps_b = _filter_taps(hid, wout_pad, decay, seg)
        spec_a = _filter_spec(dft_fwd, taps_a, nb, seg)
        spec_b = _filter_spec(dft_fwd, taps_b, nb, seg)
        w1, w2 = _dft_fwd_mul(fc, fs, uc, 0, spec_a, spec_b, 0, batch, seq, nb)
        z1 = _dft_inv_gate(fct, fst, w1, w2, uc, 0, uc, 1, hy_skip[l, 0], F32, batch * nb, seg)
        w1, w2 = _dft_fwd_mul(fc, fs, z1, 0, spec_a, spec_b, 1, batch, seq, nb)
        o_b = _dft_inv_gate(fct, fst, w1, w2, z1, 0, uc, 2, hy_skip[l, 1], BF16, batch * nb, seg)

        lat = _matmul(hb, w_lat, F32, 512, w_lat.shape[1], name="mla_latent_proj")
        wq = mla_w_uq[l].reshape(MLA_Q_RANK, MLA_HEADS, MLA_NOPE + MLA_ROPE)
        wq = jnp.pad(wq, ((0, 0), (0, 0), (0, MLA_QK_PAD - MLA_NOPE - MLA_ROPE)))
        wqt = wq.reshape(MLA_Q_RANK, MLA_HEADS * MLA_QK_PAD).T.astype(BF16)
        wkv = mla_w_ukv[l].reshape(MLA_KV_RANK, MLA_HEADS, MLA_NOPE + MLA_V)
        wk = wkv[:, :, :MLA_NOPE].reshape(MLA_KV_RANK, MLA_WIDTH).astype(BF16)
        wvt = wkv[:, :, MLA_NOPE:].reshape(MLA_KV_RANK, MLA_WIDTH).T.astype(BF16)
        mla_scale = (MLA_NOPE + MLA_ROPE) ** -0.5 * LOG2_E
        qt_c = _mla_q_t(lat, mla_q_norm_g[l], wqt, tabs_t, mla_scale)
        k_cat, vt_c = _mla_kv(lat, mla_kv_norm_g[l], wk, wvt, tabs)
        o_c = _attention_t(lam2, qt_c, k_cat, vt_c, 0, jnp.ones((MLA_V, 1), F32), 1, batch, seq,
                           MLA_HEADS, MLA_QK_PAD, MLA_V, min(seq, 4096), min(seq, 512), "mla_attention")

        merged = _merge(hb, wgs, gate_b[l], (o_a, o_b, o_c),
                        (w_branch_a[l].astype(BF16), w_branch_b[l].astype(BF16), w_branch_c[l].astype(BF16)))
        h, hb = _matmul_res_ln(merged, w_out[l].astype(BF16), h, ln1_g[l], ln1_b[l], tk=d, name="out_proj_ln")

        ff = _matmul(hb, mlp_w1[l].astype(BF16), BF16, 1024, 2048, square_relu=True, name="mlp_up")
        h, hb = _matmul_res_ln(ff, mlp_w2[l].astype(BF16), h, ln2_g[l], ln2_b[l], tk=2048, name="mlp_down_ln")

    return h.reshape(batch, seq, d)
```
